```python
import math
import jax, jax.numpy as jnp
from jax import lax
import numpy as np

D_MODEL = 1024
BATCH = 16
SEQ = 2048
DEPTH = 1
DEC_BATCH = 32
DEC_SEQ = 8
PAST_LEN = 16384
PAGE_SIZE = 128

DIFF_HEADS = 8
DIFF_QK = 32
DIFF_V = 2 * DIFF_QK
GLA_HEADS = 4
GLA_DK = 64
GLA_DV = 128
GLA_GATE_RANK = 16
GLA_TAU = 16.0
GLA_CHUNK = 64
N_MEM = 256
CROSS_HEADS = 4
CROSS_HD = D_MODEL // CROSS_HEADS
PEER_HEADS = 8
PEER_QD = 256
N_KEYS = 128
N_EXPERTS = N_KEYS * N_KEYS
PEER_TOPK = 16
PEER_BLOCK = 256
Q_BLOCK = 128
EPS = 1e-6

MIX_WIDTH = DIFF_HEADS * DIFF_V + GLA_HEADS * GLA_DV
IN_SPLIT_SIZES = (DIFF_HEADS * 2 * DIFF_QK, DIFF_HEADS * 2 * DIFF_QK, DIFF_HEADS * DIFF_V,
                  GLA_HEADS * GLA_DK, GLA_HEADS * GLA_DK, GLA_HEADS * GLA_DV,
                  GLA_GATE_RANK, GLA_HEADS * GLA_DV)
IN_WIDTH = sum(IN_SPLIT_SIZES)

kernel_name = "hymba_diffattn_gla_peer_step"


def rms_norm(x, g):
    xf = x.astype(jnp.float32)
    y = xf * lax.rsqrt(jnp.mean(xf * xf, axis=-1, keepdims=True) + EPS)
    return (y * g.astype(jnp.float32)).astype(x.dtype)


def lambda_init(layer):
    return 0.8 - 0.6 * math.exp(-0.3 * layer)


def diff_lambda(lq1, lk1, lq2, lk2, lam_init):
    f = lambda a: a.astype(jnp.float32)
    return jnp.exp(jnp.sum(f(lq1) * f(lk1))) - jnp.exp(jnp.sum(f(lq2) * f(lk2))) + lam_init


def mixer_projections(xn, w_in, q_norm, k_norm, w_gate, b_gate):
    B, T, _ = xn.shape
    offs, acc = [], 0
    for s in IN_SPLIT_SIZES[:-1]:
        acc += s
        offs.append(acc)
    dq, dk, dv, gq, gk, gv, ga, gr = jnp.split(xn @ w_in, offs, axis=-1)
    dq = rms_norm(dq.reshape(B, T, DIFF_HEADS, 2, DIFF_QK), q_norm) * (DIFF_QK ** -0.5)
    dk = rms_norm(dk.reshape(B, T, DIFF_HEADS, 2, DIFF_QK), k_norm)
    dv = dv.reshape(B, T, DIFF_HEADS, DIFF_V)
    gq = gq.reshape(B, T, GLA_HEADS, GLA_DK) * (GLA_DK ** -0.5)
    gk = gk.reshape(B, T, GLA_HEADS, GLA_DK)
    gv = gv.reshape(B, T, GLA_HEADS, GLA_DV)
    log_a = jax.nn.log_sigmoid((ga @ w_gate + b_gate).astype(jnp.float32)) / GLA_TAU
    log_a = log_a.reshape(B, T, GLA_HEADS, GLA_DK)
    return dq, dk, dv, gq, gk, gv, log_a, gr


def diff_attend(q, ks, vs, masks, lam):
    scores = []
    for k_, m in zip(ks, masks):
        s = jnp.einsum('bqhcd,bkhcd->bhcqk', q, k_.astype(q.dtype)).astype(jnp.float32)
        scores.append(s if m is None else jnp.where(m, s, -jnp.inf))
    p = jax.nn.softmax(jnp.concatenate(scores, axis=-1), axis=-1)
    p = p[:, :, 0] - lam * p[:, :, 1]
    out, off = 0.0, 0
    for k_, v_ in zip(ks, vs):
        n = k_.shape[1]
        out = out + jnp.einsum('bhqk,bkhd->bqhd', p[..., off:off + n], v_.astype(jnp.float32))
        off += n
    return out


def diff_attend_prompt(q, k, v, lam):
    B, S = q.shape[:2]
    qb = math.gcd(S, Q_BLOCK)
    n = S // qb
    q_blocks = jnp.moveaxis(q.reshape(B, n, qb, *q.shape[2:]), 1, 0)
    starts = jnp.arange(n, dtype=jnp.int32) * qb
    kpos = jnp.arange(S, dtype=jnp.int32)

    def one(args):
        qblk, start = args
        qpos = start + jnp.arange(qb, dtype=jnp.int32)
        return diff_attend(qblk, [k], [v], [kpos[None, :] <= qpos[:, None]], lam)

    o = lax.map(one, (q_blocks, starts))
    return jnp.moveaxis(o, 0, 1).reshape(B, S, DIFF_HEADS, DIFF_V)


def diff_attend_sample(q, k_new, v_new, k_past, v_past, lam):
    T = q.shape[1]
    causal = jnp.tril(jnp.ones((T, T), dtype=bool))
    return diff_attend(q, [k_past, k_new], [v_past, v_new], [None, causal], lam)


def diff_post(o, out_norm, lam_init):
    B, T = o.shape[:2]
    return (rms_norm(o, out_norm) * (1.0 - lam_init)).reshape(B, T, -1)


def gla_recurrence(q, k, v, log_a, S0):
    B, T, H, DK = q.shape
    DV = v.shape[-1]
    C = math.gcd(T, GLA_CHUNK)
    n = T // C

    def chunks(a):
        return jnp.moveaxis(a.astype(jnp.float32).reshape(B, n, C, *a.shape[2:]), 1, 0)

    tril = jnp.tril(jnp.ones((C, C), dtype=bool))[None, :, :, None, None]

    def step(S, inp):
        qc, kc, vc, ac = inp
        b = jnp.cumsum(ac, axis=1)
        rel = b[:, :, None] - b[:, None, :]
        decay = jnp.where(tril, jnp.exp(jnp.where(tril, rel, 0.0)), 0.0)
        attn = jnp.einsum('bihd,bjhd,bijhd->bhij', qc, kc, decay)
        o = (jnp.einsum('bhij,bjhe->bihe', attn, vc)
             + jnp.einsum('bihd,bhde->bihe', qc * jnp.exp(b), S))
        b_last = b[:, -1]
        S = (jnp.exp(b_last)[..., None] * S
             + jnp.einsum('bjhd,bjhe->bhde', kc * jnp.exp(b_last[:, None] - b), vc))
        return S, o

    S, o = lax.scan(step, S0.astype(jnp.float32), (chunks(q), chunks(k), chunks(v), chunks(log_a)))
    return jnp.moveaxis(o, 0, 1).reshape(B, T, H, DV), S


def gla_branch(q, k, v, log_a, r, S0, out_norm):
    o, S = gla_recurrence(q, k, v, log_a, S0)
    B, T = o.shape[:2]
    o = rms_norm(o, out_norm).reshape(B, T, -1) * jax.nn.silu(r.astype(jnp.float32))
    return o, S


def memory_kv(mem, mem_norm, w_kv, k_norm):
    B, M, _ = mem.shape
    kv = (rms_norm(mem, mem_norm) @ w_kv).reshape(B, M, 2, CROSS_HEADS, CROSS_HD)
    return rms_norm(kv[:, :, 0], k_norm), kv[:, :, 1]


def cross_attend(hn, mk, mv, w_q, q_norm, w_o):
    B, T, _ = hn.shape
    q = rms_norm((hn @ w_q).reshape(B, T, CROSS_HEADS, CROSS_HD), q_norm) * (CROSS_HD ** -0.5)
    s = jnp.einsum('bqhd,bmhd->bhqm', q, mk.astype(q.dtype)).astype(jnp.float32)
    p = jax.nn.softmax(s, axis=-1)
    o = jnp.einsum('bhqm,bmhd->bqhd', p, mv.astype(jnp.float32)).reshape(B, T, D_MODEL)
    return o.astype(hn.dtype) @ w_o


def peer_ffn(xn, w_q, q_norm, sub_keys, u, v):
    B, T, D = xn.shape
    n_tok = B * T
    blk = math.gcd(n_tok, PEER_BLOCK)
    xb = xn.reshape(n_tok // blk, blk, D)

    def one(xt):
        q = rms_norm((xt @ w_q).reshape(blk, PEER_HEADS, PEER_QD), q_norm)
        q = q.reshape(blk, PEER_HEADS, 2, PEER_QD // 2)
        s = jnp.einsum('thcd,hckd->thck', q, sub_keys).astype(jnp.float32)
        s1, i1 = lax.top_k(s[:, :, 0], PEER_TOPK)
        s2, i2 = lax.top_k(s[:, :, 1], PEER_TOPK)
        cand_s = (s1[..., :, None] + s2[..., None, :]).reshape(blk, PEER_HEADS, -1)
        cand_i = (i1[..., :, None] * N_KEYS + i2[..., None, :]).reshape(blk, PEER_HEADS, -1)
        top_s, j = lax.top_k(cand_s, PEER_TOPK)
        e = jnp.take_along_axis(cand_i, j, axis=-1).reshape(blk, -1)
        g = jax.nn.softmax(top_s, axis=-1).reshape(blk, -1)
        act = jax.nn.gelu(jnp.einsum('ted,td->te', u[e], xt).astype(jnp.float32))
        return jnp.einsum('te,ted->td', (g * act).astype(xt.dtype), v[e])

    return lax.map(one, xb).reshape(B, T, D)


def setup_inputs(seed: int = 0) -> dict:
    key = jax.random.key(seed)
    ks = iter(jax.random.split(key, 48))
    f32 = jnp.float32
    nrm = lambda shape, scale: jax.random.normal(next(ks), shape, f32) * scale
    gain = lambda shape: 1.0 + 0.02 * jax.random.normal(next(ks), shape, f32)
    n_pages = PAST_LEN // PAGE_SIZE
    n_used = DEC_BATCH * n_pages
    n_phys = n_used + max(1, n_used // 4)
    perm = jax.random.permutation(next(ks), n_phys)
    page_table = perm[:n_used].reshape(DEC_BATCH, n_pages).astype(jnp.int32)
    dsc = D_MODEL ** -0.5
    return {
        "x_prompt": nrm((BATCH, SEQ, D_MODEL), 1.0),
        "x_sample": nrm((DEC_BATCH, DEC_SEQ, D_MODEL), 1.0),
        "mem_prompt": nrm((BATCH, N_MEM, D_MODEL), 1.0),
        "cache_diff_k": nrm((DEPTH, n_phys, PAGE_SIZE, DIFF_HEADS, 2 * DIFF_QK), 1.0),
        "cache_diff_v": nrm((DEPTH, n_phys, PAGE_SIZE, DIFF_HEADS, DIFF_V), 1.0),
        "page_table": page_table,
        "state_gla": nrm((DEPTH, DEC_BATCH, GLA_HEADS, GLA_DK, GLA_DV), 1.0),
        "cache_mem_k": nrm((DEPTH, DEC_BATCH, N_MEM, CROSS_HEADS, CROSS_HD), 1.0),
        "cache_mem_v": nrm((DEPTH, DEC_BATCH, N_MEM, CROSS_HEADS, CROSS_HD), 1.0),
        "attn_norm": gain((DEPTH, D_MODEL)),
        "w_in": nrm((DEPTH, D_MODEL, IN_WIDTH), dsc),
        "diff_q_norm": gain((DEPTH, DIFF_QK)),
        "diff_k_norm": gain((DEPTH, DIFF_QK)),
        "lambda_q1": nrm((DEPTH, DIFF_QK), 0.1),
        "lambda_k1": nrm((DEPTH, DIFF_QK), 0.1),
        "lambda_q2": nrm((DEPTH, DIFF_QK), 0.1),
        "lambda_k2": nrm((DEPTH, DIFF_QK), 0.1),
        "diff_out_norm": gain((DEPTH, DIFF_V)),
        "gla_w_gate": nrm((DEPTH, GLA_GATE_RANK, GLA_HEADS * GLA_DK), GLA_GATE_RANK ** -0.5),
        "gla_b_gate": nrm((DEPTH, GLA_HEADS * GLA_DK), 0.1),
        "gla_out_norm": gain((DEPTH, GLA_DV)),
        "w_o": nrm((DEPTH, MIX_WIDTH, D_MODEL), MIX_WIDTH ** -0.5),
        "cross_norm": gain((DEPTH, D_MODEL)),
        "mem_norm": gain((DEPTH, D_MODEL)),
        "cross_w_q": nrm((DEPTH, D_MODEL, D_MODEL), dsc),
        "cross_w_kv": nrm((DEPTH, D_MODEL, 2 * D_MODEL), dsc),
        "cross_q_norm": gain((DEPTH, CROSS_HD)),
        "cross_k_norm": gain((DEPTH, CROSS_HD)),
        "cross_w_o": nrm((DEPTH, D_MODEL, D_MODEL), dsc),
        "ffn_norm": gain((DEPTH, D_MODEL)),
        "peer_w_q": nrm((DEPTH, D_MODEL, PEER_HEADS * PEER_QD), dsc),
        "peer_q_norm": gain((DEPTH, PEER_QD)),
        "peer_sub_keys": nrm((DEPTH, PEER_HEADS, 2, N_KEYS, PEER_QD // 2), (PEER_QD // 2) ** -0.5),
        "peer_u": nrm((DEPTH, N_EXPERTS, D_MODEL), dsc),
        "peer_v": nrm((DEPTH, N_EXPERTS, D_MODEL), (PEER_HEADS * PEER_TOPK) ** -0.5),
    }


def reference(x_prompt, x_sample, mem_prompt, cache_diff_k, cache_diff_v, page_table, state_gla,
              cache_mem_k, cache_mem_v, attn_norm, w_in, diff_q_norm, diff_k_norm, lambda_q1,
              lambda_k1, lambda_q2, lambda_k2, diff_out_norm, gla_w_gate, gla_b_gate, gla_out_norm,
              w_o, cross_norm, mem_norm, cross_w_q, cross_w_kv, cross_q_norm, cross_k_norm,
              cross_w_o, ffn_norm, peer_w_q, peer_q_norm, peer_sub_keys, peer_u, peer_v):
    B, S, _ = x_prompt.shape
    DB, T, _ = x_sample.shape
    n_pages = page_table.shape[1]
    past_len = n_pages * cache_diff_k.shape[2]
    hp, hs = x_prompt, x_sample
    pk, pv, pst, pmk, pmv, sk, sv, sst = [], [], [], [], [], [], [], []
    for l in range(DEPTH):
        lam_init = lambda_init(l)
        lam = diff_lambda(lambda_q1[l], lambda_k1[l], lambda_q2[l], lambda_k2[l], lam_init)

        xn = rms_norm(hp, attn_norm[l])
        dq, dk, dv, gq, gk, gv, la, gr = mixer_projections(
            xn, w_in[l], diff_q_norm[l], diff_k_norm[l], gla_w_gate[l], gla_b_gate[l])
        o_d = diff_post(diff_attend_prompt(dq, dk, dv, lam), diff_out_norm[l], lam_init)
        S0 = jnp.zeros((B, GLA_HEADS, GLA_DK, GLA_DV), jnp.float32)
        o_g, S_p = gla_branch(gq, gk, gv, la, gr, S0, gla_out_norm[l])
        hp = hp + jnp.concatenate([o_d, o_g], axis=-1).astype(hp.dtype) @ w_o[l]
        mk_p, mv_p = memory_kv(mem_prompt, mem_norm[l], cross_w_kv[l], cross_k_norm[l])
        hp = hp + cross_attend(rms_norm(hp, cross_norm[l]), mk_p, mv_p,
                               cross_w_q[l], cross_q_norm[l], cross_w_o[l])
        hp = hp + peer_ffn(rms_norm(hp, ffn_norm[l]), peer_w_q[l], peer_q_norm[l],
                           peer_sub_keys[l], peer_u[l], peer_v[l])
        pk.append(dk.reshape(B, S, DIFF_HEADS, 2 * DIFF_QK))
        pv.append(dv)
        pst.append(S_p.astype(x_prompt.dtype))
        pmk.append(mk_p)
        pmv.append(mv_p)

        xn = rms_norm(hs, attn_norm[l])
        dq, dk, dv, gq, gk, gv, la, gr = mixer_projections(
            xn, w_in[l], diff_q_norm[l], diff_k_norm[l], gla_w_gate[l], gla_b_gate[l])
        k_past = cache_diff_k[l, page_table].reshape(DB, past_len, DIFF_HEADS, 2, DIFF_QK)
        v_past = cache_diff_v[l, page_table].reshape(DB, past_len, DIFF_HEADS, DIFF_V)
        o_d = diff_post(diff_attend_sample(dq, dk, dv, k_past, v_past, lam), diff_out_norm[l], lam_init)
        o_g, S_s = gla_branch(gq, gk, gv, la, gr, state_gla[l], gla_out_norm[l])
        hs = hs + jnp.concatenate([o_d, o_g], axis=-1).astype(hs.dtype) @ w_o[l]
        hs = hs + cross_attend(rms_norm(hs, cross_norm[l]), cache_mem_k[l], cache_mem_v[l],
                               cross_w_q[l], cross_q_norm[l], cross_w_o[l])
        hs = hs + peer_ffn(rms_norm(hs, ffn_norm[l]), peer_w_q[l], peer_q_norm[l],
                           peer_sub_keys[l], peer_u[l], peer_v[l])
        sk.append(dk.reshape(DB, T, DIFF_HEADS, 2 * DIFF_QK))
        sv.append(dv)
        sst.append(S_s.astype(x_sample.dtype))

    return (hp, hs, jnp.stack(pk), jnp.stack(pv), jnp.stack(pst), jnp.stack(pmk), jnp.stack(pmv),
            jnp.stack(sk), jnp.stack(sv), jnp.stack(sst))
```

```python
import functools
import math

import jax
import jax.numpy as jnp
from jax import lax
from jax.experimental import pallas as pl
from jax.experimental.pallas import tpu as pltpu

F32 = jnp.float32
BF16 = jnp.bfloat16
I32 = jnp.int32

EPS = 1e-6
D_MODEL = 1024
DIFF_HEADS = 8
DIFF_QK = 32
DIFF_V = 64
DIFF_W = DIFF_HEADS * DIFF_V
GLA_HEADS = 4
GLA_DK = 64
GLA_DV = 128
GLA_QW = GLA_HEADS * GLA_DK
GLA_VW = GLA_HEADS * GLA_DV
GLA_GATE_RANK = 16
GLA_TAU = 16.0
GLA_SUB = 16
CROSS_HEADS = 4
CROSS_HD = 256
N_MEM = 256
PEER_HEADS = 8
PEER_QD = 256
N_KEYS = 128
PEER_TOPK = 16
PEER_SEL = PEER_HEADS * PEER_TOPK
PAGE_SIZE = 128
LANES = 128
VMEM_LIMIT = 56 * 1024 * 1024

_OFF_GA = 2560
_OFF_GR = 2576


def _params(sem, vmem=None):
    return pltpu.CompilerParams(dimension_semantics=sem, vmem_limit_bytes=vmem)


def _rms(x, g):
    return x * lax.rsqrt(jnp.mean(x * x, axis=-1, keepdims=True) + EPS) * g


def _split_dot(x, w):
    hi = x.astype(BF16)
    lo = (x - hi.astype(F32)).astype(BF16)
    return (jnp.dot(hi, w, preferred_element_type=F32)
            + jnp.dot(lo, w, preferred_element_type=F32))


def _group_rms(x, bd, g):
    ms = _split_dot(x * x, bd)
    return x * lax.rsqrt(ms + EPS) * g


def _block_diag(width, group, value, dtype=BF16):
    i = jnp.arange(width) // group
    return jnp.where(i[:, None] == i[None, :], value, 0.0).astype(dtype)


def _proj_in_kernel(x_ref, g_ref, wa_ref, wgr_ref, wga_ref, qn_ref, kn_ref, bd_ref, wg_ref, bg_ref,
                    dq_ref, dk_ref, dv_ref, dk16_ref, dv16_ref, gq_ref, gk_ref, gv_ref, la_ref,
                    gr_ref):
    xb = _rms(x_ref[...], g_ref[...]).astype(BF16)
    ya = jnp.dot(xb, wa_ref[...], preferred_element_type=F32)
    bd = bd_ref[...]
    dq = _group_rms(ya[:, 0:512], bd, qn_ref[...])
    dk = _group_rms(ya[:, 512:1024], bd, kn_ref[...])
    dv = ya[:, 1024:1536]
    dq_ref[...] = dq.astype(BF16)
    dk_ref[...] = dk
    dv_ref[...] = dv
    dk16_ref[...] = dk.astype(BF16)
    dv16_ref[...] = dv.astype(BF16)
    gq_ref[...] = ya[:, 1536:1792] * (GLA_DK ** -0.5)
    gk_ref[...] = ya[:, 1792:2048]
    gv_ref[...] = ya[:, 2048:2560]
    gr_ref[...] = jnp.dot(xb, wgr_ref[...], preferred_element_type=F32)
    ga = jnp.dot(xb, wga_ref[...], preferred_element_type=F32)
    z = jnp.dot(ga.astype(BF16), wg_ref[...], preferred_element_type=F32) + bg_ref[...]
    log_sig = jnp.minimum(z, 0.0) - jnp.log(1.0 + jnp.exp(-jnp.abs(z)))
    la_ref[...] = log_sig * (1.0 / GLA_TAU)


def _proj_in(x, p, tm=256):
    n = x.shape[0]
    tm = min(tm, n)
    row = lambda w: pl.BlockSpec((tm, w), lambda i: (i, 0))
    full = lambda a: pl.BlockSpec(a.shape, lambda i: (0,) * a.ndim)
    consts = (p["attn_norm"], p["w_a"], p["w_gr"], p["w_ga"], p["dq_norm"], p["dk_norm"],
              p["bd32"], p["w_gate"], p["b_gate"])
    out_w = ((512, BF16), (512, F32), (512, F32), (512, BF16), (512, BF16), (256, F32), (256, F32),
             (512, F32), (256, F32), (512, F32))
    return pl.pallas_call(
        _proj_in_kernel,
        grid=(n // tm,),
        in_specs=[row(D_MODEL)] + [full(a) for a in consts],
        out_specs=[row(w) for w, _ in out_w],
        out_shape=[jax.ShapeDtypeStruct((n, w), dt) for w, dt in out_w],
        compiler_params=_params(("parallel",), VMEM_LIMIT),
        name="proj_in",
    )(x, *consts)


def _mix_out_kernel(od_ref, og_ref, gr_ref, x_ref, gn_ref, wod_ref, wog_ref, cn_ref, wq_ref,
                    qn_ref, h_ref, qc_ref):
    og = og_ref[...]
    gr = gr_ref[...]
    gn = gn_ref[...]
    parts = []
    for h in range(GLA_HEADS):
        sl = slice(h * GLA_DV, (h + 1) * GLA_DV)
        parts.append(_rms(og[:, sl], gn[:, sl]))
    ogn = jnp.concatenate(parts, axis=-1) * (gr * jax.nn.sigmoid(gr))
    mix = (jnp.dot(od_ref[...], wod_ref[...], preferred_element_type=F32)
           + jnp.dot(ogn.astype(BF16), wog_ref[...], preferred_element_type=F32))
    h1 = x_ref[...] + mix
    h_ref[...] = h1
    hn = _rms(h1, cn_ref[...]).astype(BF16)
    q = jnp.dot(hn, wq_ref[...], preferred_element_type=F32)
    qn = qn_ref[...]
    parts = []
    for h in range(CROSS_HEADS):
        sl = slice(h * CROSS_HD, (h + 1) * CROSS_HD)
        parts.append(_rms(q[:, sl], qn[:, sl]))
    qc_ref[...] = jnp.concatenate(parts, axis=-1).astype(BF16)


def _mix_out(od, og, gr, x, p, tm=256):
    n = x.shape[0]
    tm = min(tm, n)
    row = lambda w: pl.BlockSpec((tm, w), lambda i: (i, 0))
    full = lambda a: pl.BlockSpec(a.shape, lambda i: (0,) * a.ndim)
    consts = (p["gla_out_norm"], p["w_o_d"], p["w_o_g"], p["cross_norm"], p["cross_w_q"],
              p["cross_q_norm"])
    return pl.pallas_call(
        _mix_out_kernel,
        grid=(n // tm,),
        in_specs=[row(512), row(512), row(512), row(D_MODEL)] + [full(a) for a in consts],
        out_specs=[row(D_MODEL), row(D_MODEL)],
        out_shape=[jax.ShapeDtypeStruct((n, D_MODEL), F32), jax.ShapeDtypeStruct((n, D_MODEL), BF16)],
        compiler_params=_params(("parallel",), VMEM_LIMIT),
        name="mix_out",
    )(od, og, gr, x, *consts)


def _mem_kv_kernel(m_ref, mn_ref, wkv_ref, kn_ref, k_ref, v_ref):
    mb = _rms(m_ref[...], mn_ref[...]).astype(BF16)
    kv = jnp.dot(mb, wkv_ref[...], preferred_element_type=F32)
    kn = kn_ref[...]
    parts = []
    for h in range(CROSS_HEADS):
        sl = slice(h * CROSS_HD, (h + 1) * CROSS_HD)
        parts.append(_rms(kv[:, sl], kn[:, sl]))
    k_ref[...] = jnp.concatenate(parts, axis=-1)
    v_ref[...] = kv[:, D_MODEL:]


def _mem_kv(mem, p, tm=256):
    n = mem.shape[0]
    tm = min(tm, n)
    row = lambda w: pl.BlockSpec((tm, w), lambda i: (i, 0))
    full = lambda a: pl.BlockSpec(a.shape, lambda i: (0,) * a.ndim)
    consts = (p["mem_norm"], p["cross_w_kv"], p["cross_k_norm"])
    return pl.pallas_call(
        _mem_kv_kernel,
        grid=(n // tm,),
        in_specs=[row(D_MODEL)] + [full(a) for a in consts],
        out_specs=[row(D_MODEL), row(D_MODEL)],
        out_shape=[jax.ShapeDtypeStruct((n, D_MODEL), F32)] * 2,
        compiler_params=_params(("parallel",), VMEM_LIMIT),
        name="mem_kv",
    )(mem, *consts)


def _cross_kernel(qc_ref, mk_ref, mv_ref, h1_ref, wo_ref, fn_ref, wpq_ref, pqn_ref,
                  h2_ref, xt_ref, pq_ref):
    qc = qc_ref[0]
    mk = mk_ref[0].astype(BF16)
    mv = mv_ref[0].astype(BF16)
    parts = []
    for h in range(CROSS_HEADS):
        sl = slice(h * CROSS_HD, (h + 1) * CROSS_HD)
        s = lax.dot_general(qc[:, sl], mk[:, sl], (((1,), (1,)), ((), ())),
                            preferred_element_type=F32)
        e = jnp.exp(s - jnp.max(s, axis=-1, keepdims=True))
        prob = e / jnp.sum(e, axis=-1, keepdims=True)
        parts.append(jnp.dot(prob.astype(BF16), mv[:, sl], preferred_element_type=F32))
    o = jnp.concatenate(parts, axis=-1).astype(BF16)
    h2 = h1_ref[0] + jnp.dot(o, wo_ref[...], preferred_element_type=F32)
    h2_ref[0] = h2
    xt = _rms(h2, fn_ref[...])
    xt_ref[0] = xt
    pq = jnp.dot(xt.astype(BF16), wpq_ref[...], preferred_element_type=F32)
    pqn = pqn_ref[...]
    parts = []
    for h in range(PEER_HEADS):
        sl = slice(h * PEER_QD, (h + 1) * PEER_QD)
        parts.append(_rms(pq[:, sl], pqn[:, sl]))
    pq_ref[0] = jnp.concatenate(parts, axis=-1).astype(BF16)


def _cross(qc, mk, mv, h1, p, tq=256):
    b, t, _ = qc.shape
    tq = min(tq, t)
    row = lambda w: pl.BlockSpec((1, tq, w), lambda i, j: (i, j, 0))
    mem = pl.BlockSpec((1, N_MEM, D_MODEL), lambda i, j: (i, 0, 0))
    full = lambda a: pl.BlockSpec(a.shape, lambda i, j: (0,) * a.ndim)
    consts = (p["cross_w_o"], p["ffn_norm"], p["peer_w_q"], p["peer_q_norm"])
    return pl.pallas_call(
        _cross_kernel,
        grid=(b, t // tq),
        in_specs=[row(D_MODEL), mem, mem, row(D_MODEL)] + [full(a) for a in consts],
        out_specs=[row(D_MODEL), row(D_MODEL), row(2048)],
        out_shape=[jax.ShapeDtypeStruct((b, t, D_MODEL), F32),
                   jax.ShapeDtypeStruct((b, t, D_MODEL), F32),
                   jax.ShapeDtypeStruct((b, t, 2048), BF16)],
        compiler_params=_params(("parallel", "parallel"), VMEM_LIMIT),
        name="cross_attn",
    )(qc, mk, mv, h1, *consts)


def _head_pair_norm(o, g):
    lane = lax.broadcasted_iota(I32, o.shape, 1)
    left = lane < DIFF_V
    sq = o * o
    ms_l = jnp.sum(jnp.where(left, sq, 0.0), axis=-1, keepdims=True)
    ms_r = jnp.sum(jnp.where(left, 0.0, sq), axis=-1, keepdims=True)
    ms = jnp.where(left, ms_l, ms_r) * (1.0 / DIFF_V)
    return o * lax.rsqrt(ms + EPS) * g


def _diff_prompt_kernel(lam_ref, q_ref, k_ref, v_ref, on_ref, o_ref, *, tq):
    qi = pl.program_id(1)
    lam = lam_ref[0, 0]
    lane = lax.broadcasted_iota(I32, (tq, LANES), 1)
    row = qi * tq + lax.broadcasted_iota(I32, (tq, tq), 0)
    col0 = lax.broadcasted_iota(I32, (tq, tq), 1)
    for p in range(DIFF_HEADS // 2):
        sl = slice(p * LANES, (p + 1) * LANES)
        qp = q_ref[0, :, sl]
        outs = []
        for j in range(4):
            qm = jnp.where((lane >= DIFF_QK * j) & (lane < DIFF_QK * (j + 1)), qp,
                           jnp.zeros_like(qp))

            def body(kb, carry, qm=qm):
                m, l, acc = carry
                t0 = pl.multiple_of(kb * tq, tq)
                kblk = k_ref[0, pl.ds(t0, tq), sl]
                vblk = v_ref[0, pl.ds(t0, tq), sl]
                s = lax.dot_general(qm, kblk, (((1,), (1,)), ((), ())),
                                    preferred_element_type=F32)
                s = jnp.where(col0 + kb * tq <= row, s, -jnp.inf)
                m_new = jnp.maximum(m, jnp.max(s, axis=-1, keepdims=True))
                alpha = jnp.exp(m - m_new)
                e = jnp.exp(s - m_new)
                l = alpha * l + jnp.sum(e, axis=-1, keepdims=True)
                acc = alpha * acc + jnp.dot(e.astype(BF16), vblk, preferred_element_type=F32)
                return m_new, l, acc

            init = (jnp.full((tq, 1), -jnp.inf, F32), jnp.zeros((tq, 1), F32),
                    jnp.zeros((tq, LANES), F32))
            m, l, acc = lax.fori_loop(0, qi + 1, body, init)
            outs.append(acc / l)
        o = jnp.where(lane < DIFF_V, outs[0] - lam * outs[1], outs[2] - lam * outs[3])
        o_ref[0, :, sl] = _head_pair_norm(o, on_ref[:, sl]).astype(BF16)


def _diff_prompt(lam, dq, dk16, dv16, p, tq=256):
    b, s, _ = dq.shape
    tq = min(tq, s)
    return pl.pallas_call(
        functools.partial(_diff_prompt_kernel, tq=tq),
        grid=(b, s // tq),
        in_specs=[pl.BlockSpec(memory_space=pltpu.SMEM),
                  pl.BlockSpec((1, tq, DIFF_W), lambda i, j: (i, j, 0)),
                  pl.BlockSpec((1, s, DIFF_W), lambda i, j: (i, 0, 0)),
                  pl.BlockSpec((1, s, DIFF_W), lambda i, j: (i, 0, 0)),
                  pl.BlockSpec((1, DIFF_W), lambda i, j: (0, 0))],
        out_specs=pl.BlockSpec((1, tq, DIFF_W), lambda i, j: (i, j, 0)),
        out_shape=jax.ShapeDtypeStruct((b, s, DIFF_W), BF16),
        compiler_params=_params(("parallel", "parallel"), VMEM_LIMIT),
        name="diff_prompt",
    )(lam, dq, dk16, dv16, p["diff_out_norm"])


def _gla_kernel(q_ref, k_ref, v_ref, la_ref, st0_ref, tri_ref, ones_ref, o_ref, stout_ref,
                st_scr, b_scr, *, ts):
    si = pl.program_id(1)

    @pl.when(si == 0)
    def _():
        st_scr[...] = st0_ref[0]

    la = la_ref[0]
    hi = la.astype(BF16)
    r1 = la - hi.astype(F32)
    mid = r1.astype(BF16)
    lo = (r1 - mid.astype(F32)).astype(BF16)
    tri = tri_ref[...]
    b = (jnp.dot(tri, hi, preferred_element_type=F32) + jnp.dot(tri, mid, preferred_element_type=F32)
         + jnp.dot(tri, lo, preferred_element_type=F32))
    b_scr[...] = b

    q = q_ref[0]
    k = k_ref[0]
    v = v_ref[0]
    sub_q = lax.broadcasted_iota(I32, (ts, GLA_QW), 0) % GLA_SUB
    sub_v = lax.broadcasted_iota(I32, (ts, GLA_VW), 0) % GLA_SUB
    ones = ones_ref[...]
    o_intra = jnp.zeros((ts, GLA_VW), F32)
    for d in range(GLA_SUB):
        if d == 0:
            kd, bd, vd = k, b, v
        else:
            kd = pltpu.roll(k, d, 0)
            bd = pltpu.roll(b, d, 0)
            vd = pltpu.roll(v, d, 0)
        ok = sub_q >= d
        e = jnp.where(ok, q * kd * jnp.exp(jnp.where(ok, b - bd, 0.0)), 0.0)
        w = jnp.dot(e.astype(BF16), ones, preferred_element_type=F32)
        o_intra = o_intra + w * jnp.where(sub_v >= d, vd, 0.0)
    o_ref[0] = o_intra

    lane = lax.broadcasted_iota(I32, (GLA_SUB, LANES), 1)
    lane_sq = lax.broadcasted_iota(I32, (LANES, LANES), 1)

    def step(c, carry):
        t0 = pl.multiple_of(c * GLA_SUB, GLA_SUB)
        bc = b_scr[pl.ds(t0, GLA_SUB), :]
        qc = q_ref[0, pl.ds(t0, GLA_SUB), :]
        kc = k_ref[0, pl.ds(t0, GLA_SUB), :]
        vc = v_ref[0, pl.ds(t0, GLA_SUB), :]
        bl = bc[GLA_SUB - 1:GLA_SUB, :]
        qe = qc * jnp.exp(bc)
        ke = (kc * jnp.exp(bl - bc)).astype(BF16)
        dec = jnp.exp(bl)
        o_parts = []
        for p in range(GLA_HEADS // 2):
            sl = slice(p * LANES, (p + 1) * LANES)
            st = st_scr[p]
            stb = st.astype(BF16)
            qp = qe[:, sl]
            upd = []
            for hh in range(2):
                mine = (lane < GLA_DK) if hh == 0 else (lane >= GLA_DK)
                qm = jnp.where(mine, qp, 0.0).astype(BF16)
                o_parts.append(lax.dot_general(qm, stb, (((1,), (1,)), ((), ())),
                                               preferred_element_type=F32))
                vh = vc[:, (2 * p + hh) * GLA_DV:(2 * p + hh + 1) * GLA_DV]
                upd.append(jnp.dot(vh.T.astype(BF16), ke[:, sl], preferred_element_type=F32))
            st_scr[p] = dec[:, sl] * st + jnp.where(lane_sq < GLA_DK, upd[0], upd[1])
        o_ref[0, pl.ds(t0, GLA_SUB), :] += jnp.concatenate(o_parts, axis=-1)
        return carry

    lax.fori_loop(0, ts // GLA_SUB, step, 0)

    @pl.when(si == pl.num_programs(1) - 1)
    def _():
        stout_ref[0] = st_scr[...]


def _gla(gq, gk, gv, la, st0, p, ts=256):
    b, t, _ = gq.shape
    ts = min(ts, t)
    i = jnp.arange(ts)
    tri = ((i[:, None] // GLA_SUB == i[None, :] // GLA_SUB) & (i[None, :] <= i[:, None])).astype(BF16)
    ones = _block_diag_rect(GLA_QW, GLA_DK, GLA_VW, GLA_DV)
    seq = lambda w: pl.BlockSpec((1, ts, w), lambda bi, si: (bi, si, 0))
    stspec = pl.BlockSpec((1, GLA_HEADS // 2, LANES, LANES), lambda bi, si: (bi, 0, 0, 0))
    return pl.pallas_call(
        functools.partial(_gla_kernel, ts=ts),
        grid=(b, t // ts),
        in_specs=[seq(GLA_QW), seq(GLA_QW), seq(GLA_VW), seq(GLA_QW), stspec,
                  pl.BlockSpec((ts, ts), lambda bi, si: (0, 0)),
                  pl.BlockSpec((GLA_QW, GLA_VW), lambda bi, si: (0, 0))],
        out_specs=[seq(GLA_VW), stspec],
        out_shape=[jax.ShapeDtypeStruct((b, t, GLA_VW), F32),
                   jax.ShapeDtypeStruct((b, GLA_HEADS // 2, LANES, LANES), F32)],
        scratch_shapes=[pltpu.VMEM((GLA_HEADS // 2, LANES, LANES), F32),
                        pltpu.VMEM((ts, GLA_QW), F32)],
        compiler_params=_params(("parallel", "arbitrary"), VMEM_LIMIT),
        name="gla",
    )(gq, gk, gv, la, st0, tri, ones)


PAGES_PER_STEP = 8
NEW_T = 8


def _diff_sample_kernel(pt_ref, lam_ref, q_ref, *rest):
    del pt_ref
    np_ = PAGES_PER_STEP
    k_refs = rest[:np_]
    v_refs = rest[np_:2 * np_]
    kn_ref, vn_ref, on_ref, o_ref, m_scr, l_scr, acc_scr = rest[2 * np_:]
    i = pl.program_id(1)

    @pl.when(i == 0)
    def _():
        m_scr[...] = jnp.full(m_scr.shape, -jnp.inf, F32)
        l_scr[...] = jnp.zeros(l_scr.shape, F32)
        acc_scr[...] = jnp.zeros(acc_scr.shape, F32)

    q = q_ref[0]

    def update(scores, values):
        m_old = m_scr[...]
        m_new = m_old
        for s in scores:
            m_new = jnp.maximum(m_new, jnp.max(s, axis=-1, keepdims=True))
        alpha = jnp.exp(m_old - m_new)
        l = alpha * l_scr[...]
        acc = alpha * acc_scr[...]
        for s, v in zip(scores, values):
            e = jnp.exp(s - m_new)
            l = l + jnp.sum(e, axis=-1, keepdims=True)
            acc = acc + jnp.dot(e.astype(BF16), v, preferred_element_type=F32)
        m_scr[...] = m_new
        l_scr[...] = l
        acc_scr[...] = acc

    nt = (((1,), (1,)), ((), ()))
    scores = [lax.dot_general(q, k_ref[0].astype(BF16), nt, preferred_element_type=F32)
              for k_ref in k_refs]
    update(scores, [v_ref[0].astype(BF16) for v_ref in v_refs])

    @pl.when(i == pl.num_programs(1) - 1)
    def _():
        s = lax.dot_general(q, kn_ref[0], nt, preferred_element_type=F32)
        t_q = lax.broadcasted_iota(I32, s.shape, 0) % NEW_T
        t_k = lax.broadcasted_iota(I32, s.shape, 1)
        update([jnp.where(t_k <= t_q, s, -jnp.inf)], [vn_ref[0]])
        accn = acc_scr[...] / l_scr[...]
        rows = 2 * NEW_T
        lane = lax.broadcasted_iota(I32, (rows, DIFF_W), 1)
        o16 = jnp.zeros((rows, DIFF_W), F32)
        for h in range(DIFF_HEADS):
            o16 = jnp.where(lane // DIFF_V == h, accn[h * rows:(h + 1) * rows, :], o16)
        o = o16[:NEW_T] - lam_ref[0, 0] * o16[NEW_T:]
        lane = lane[:NEW_T]
        sq = o * o
        ms = jnp.zeros_like(o)
        for h in range(DIFF_HEADS):
            mine = lane // DIFF_V == h
            ms = jnp.where(mine, jnp.sum(jnp.where(mine, sq, 0.0), axis=-1, keepdims=True), ms)
        o_ref[0] = o * lax.rsqrt(ms * (1.0 / DIFF_V) + EPS) * on_ref[...]


def _diff_sample(lam, page_table, dq, dk16, dv16, cache_k, cache_v, p):
    db, t, _ = dq.shape
    n_pages = page_table.shape[1]
    n_phys = cache_k.shape[0]
    np_ = PAGES_PER_STEP
    a = dq.reshape(db, t, 16, DIFF_QK).transpose(0, 2, 1, 3)
    q_bd = (a[:, :, :, None, :] * jnp.eye(16, dtype=BF16)[None, :, None, :, None]
            ).reshape(db, 16 * t, DIFF_W)
    pad = lambda x: jnp.pad(x, ((0, 0), (0, LANES - t), (0, 0)))
    ck = cache_k.reshape(n_phys, PAGE_SIZE, DIFF_W)
    cv = cache_v.reshape(n_phys, PAGE_SIZE, DIFF_W)

    def page_spec(n):
        return pl.BlockSpec((1, PAGE_SIZE, DIFF_W), lambda b, i, pt: (pt[b, i * np_ + n], 0, 0))

    per_b = lambda r: pl.BlockSpec((1, r, DIFF_W), lambda b, i, pt: (b, 0, 0))
    grid_spec = pltpu.PrefetchScalarGridSpec(
        num_scalar_prefetch=1,
        grid=(db, n_pages // np_),
        in_specs=[pl.BlockSpec(memory_space=pltpu.SMEM), per_b(16 * t)]
        + [page_spec(n) for n in range(np_)] * 2
        + [per_b(LANES), per_b(LANES), pl.BlockSpec((1, DIFF_W), lambda b, i, pt: (0, 0))],
        out_specs=per_b(t),
        scratch_shapes=[pltpu.VMEM((16 * t, 1), F32), pltpu.VMEM((16 * t, 1), F32),
                        pltpu.VMEM((16 * t, DIFF_W), F32)],
    )
    return pl.pallas_call(
        _diff_sample_kernel,
        grid_spec=grid_spec,
        out_shape=jax.ShapeDtypeStruct((db, t, DIFF_W), F32),
        compiler_params=_params(("parallel", "arbitrary"), VMEM_LIMIT),
        name="diff_sample",
    )(page_table, lam, q_bd, *([ck] * np_), *([cv] * np_), pad(dk16), pad(dv16),
      p["diff_out_norm"])


def _top16_rows(s, row):
    vals, idxs = [], []
    sentinel = s.shape[0]
    for _ in range(PEER_TOPK):
        m = jnp.max(s, axis=0, keepdims=True)
        idx = jnp.min(jnp.where(s == m, row, sentinel), axis=0, keepdims=True)
        vals.append(m)
        idxs.append(idx)
        s = jnp.where(row == idx, -jnp.inf, s)
    return jnp.concatenate(vals, axis=0), jnp.concatenate(idxs, axis=0)


def _pick_row(table, row, idx):
    return jnp.max(jnp.where(row == idx, table, -1), axis=0, keepdims=True)


def _route_kernel(pq_ref, sk_ref, e_ref, g_ref, *, tb):
    nt = (((1,), (1,)), ((), ()))
    row_k = lax.broadcasted_iota(I32, (N_KEYS, tb), 0)
    row_c = lax.broadcasted_iota(I32, (PEER_TOPK * PEER_TOPK, tb), 0)
    row_t = lax.broadcasted_iota(I32, (PEER_TOPK, tb), 0)
    e_rows, g_rows = [], []
    for h in range(PEER_HEADS):
        tops = []
        for c in range(2):
            hc = 2 * h + c
            s = lax.dot_general(sk_ref[hc], pq_ref[:, hc * N_KEYS:(hc + 1) * N_KEYS], nt,
                                preferred_element_type=F32)
            tops.append(_top16_rows(s, row_k))
        (s1, i1), (s2, i2) = tops
        cand = jnp.concatenate(
            [jnp.broadcast_to(s1[a:a + 1], (PEER_TOPK, tb)) + s2 for a in range(PEER_TOPK)], axis=0)
        top_s, j = _top16_rows(cand, row_c)
        e_h = []
        for r in range(PEER_TOPK):
            jr = j[r:r + 1]
            e_h.append(_pick_row(i1, row_t, jr // PEER_TOPK) * N_KEYS
                       + _pick_row(i2, row_t, jr % PEER_TOPK))
        e_rows.append(jnp.concatenate(e_h, axis=0))
        ex = jnp.exp(top_s - top_s[0:1])
        g_rows.append(ex / jnp.sum(ex, axis=0, keepdims=True))
    e_ref[...] = jnp.concatenate(e_rows, axis=0).T
    g_ref[...] = jnp.concatenate(g_rows, axis=0).T


def _route(pq, p, tb=128):
    n = pq.shape[0]
    tb = min(tb, n)
    return pl.pallas_call(
        functools.partial(_route_kernel, tb=tb),
        grid=(n // tb,),
        in_specs=[pl.BlockSpec((tb, 2048), lambda i: (i, 0)),
                  pl.BlockSpec((2 * PEER_HEADS, N_KEYS, N_KEYS), lambda i: (0, 0, 0))],
        out_specs=[pl.BlockSpec((tb, PEER_SEL), lambda i: (i, 0))] * 2,
        out_shape=[jax.ShapeDtypeStruct((n, PEER_SEL), I32), jax.ShapeDtypeStruct((n, PEER_SEL), F32)],
        compiler_params=_params(("parallel",), VMEM_LIMIT),
        name="peer_route",
    )(pq, p["peer_sub_keys"])


HALF_ROWS = 4


def _unpack_pair(row):
    lo = lax.bitcast_convert_type(row << 16, F32)
    hi = lax.bitcast_convert_type(row & jnp.uint32(0xFFFF0000), F32)
    return lo, hi


def _gelu_tanh(x):
    return 0.5 * x * (1.0 + jnp.tanh(math.sqrt(2.0 / math.pi) * (x + 0.044715 * (x * x * x))))


def _peer_dot_kernel(e_ref, xt_ref, g_ref, tab_ref, w_ref, *, tb):
    lane = lax.broadcasted_iota(I32, (PEER_SEL, tb), 1)

    def token(t, ht):
        x = xt_ref[t]
        xl = x[:HALF_ROWS]
        xh = x[HALF_ROWS:]
        rows = []
        for j in range(PEER_SEL):
            lo, hi = _unpack_pair(tab_ref[e_ref[t, j]])
            rows.append(jnp.sum(lo * xl + hi * xh, axis=0, keepdims=True))
        dots = jnp.sum(jnp.concatenate(rows, axis=0), axis=1, keepdims=True)
        return jnp.where(lane == t, dots, ht)

    ht = lax.fori_loop(0, tb, token, jnp.zeros((PEER_SEL, tb), F32))
    w_ref[...] = g_ref[...] * _gelu_tanh(ht.T)


def _peer_dot(e, xt3, g, tab, tb=128):
    n = e.shape[0]
    tb = min(tb, n)
    return pl.pallas_call(
        functools.partial(_peer_dot_kernel, tb=tb),
        grid=(n // tb,),
        in_specs=[pl.BlockSpec((tb, PEER_SEL), lambda i: (i, 0), memory_space=pltpu.SMEM),
                  pl.BlockSpec((tb, 8, LANES), lambda i: (i, 0, 0)),
                  pl.BlockSpec((tb, PEER_SEL), lambda i: (i, 0)),
                  pl.BlockSpec(memory_space=pltpu.VMEM)],
        out_specs=pl.BlockSpec((tb, PEER_SEL), lambda i: (i, 0)),
        out_shape=jax.ShapeDtypeStruct((n, PEER_SEL), F32),
        compiler_params=_params(("arbitrary",), VMEM_LIMIT),
        name="peer_dot",
    )(e, xt3, g, tab)


def _peer_sum_kernel(e_ref, w_ref, h_ref, tab_ref, y_ref, *, tb):
    def token(t, carry):
        acc_lo = jnp.zeros((HALF_ROWS, LANES), F32)
        acc_hi = jnp.zeros((HALF_ROWS, LANES), F32)
        for j in range(PEER_SEL):
            lo, hi = _unpack_pair(tab_ref[e_ref[t, j]])
            w = w_ref[t, j]
            acc_lo = acc_lo + w * lo
            acc_hi = acc_hi + w * hi
        y_ref[t] = h_ref[t] + jnp.concatenate([acc_lo, acc_hi], axis=0)
        return carry

    lax.fori_loop(0, tb, token, 0)


def _peer_sum(e, w, h3, tab, tb=128):
    n = e.shape[0]
    tb = min(tb, n)
    smem = lambda: pl.BlockSpec((tb, PEER_SEL), lambda i: (i, 0), memory_space=pltpu.SMEM)
    tok = lambda: pl.BlockSpec((tb, 8, LANES), lambda i: (i, 0, 0))
    return pl.pallas_call(
        functools.partial(_peer_sum_kernel, tb=tb),
        grid=(n // tb,),
        in_specs=[smem(), smem(), tok(), pl.BlockSpec(memory_space=pltpu.VMEM)],
        out_specs=tok(),
        out_shape=jax.ShapeDtypeStruct((n, 8, LANES), F32),
        compiler_params=_params(("arbitrary",), VMEM_LIMIT),
        name="peer_sum",
    )(e, w, h3, tab)


def _pack_table(w):
    bits = lax.bitcast_convert_type(w.astype(BF16), jnp.uint16).astype(jnp.uint32)
    half = D_MODEL // 2
    return (bits[:, :half] | (bits[:, half:] << 16)).reshape(w.shape[0], HALF_ROWS, LANES)


def _block_diag_rect(rows, rgroup, cols, cgroup):
    r = jnp.arange(rows) // rgroup
    c = jnp.arange(cols) // cgroup
    return (r[:, None] == c[None, :]).astype(BF16)


def _state_to_pairs(s):
    b = s.shape[0]
    return s.reshape(b, 2, 2, GLA_DK, GLA_DV).transpose(0, 1, 4, 2, 3).reshape(b, 2, LANES, LANES)


def _pairs_to_state(st):
    b = st.shape[0]
    return st.reshape(b, 2, GLA_DV, 2, GLA_DK).transpose(0, 1, 3, 4, 2).reshape(
        b, GLA_HEADS, GLA_DK, GLA_DV)


def _lambda_init(layer):
    return 0.8 - 0.6 * math.exp(-0.3 * layer)


def _prep_layer(l, w):
    row = lambda a: a.reshape(1, -1).astype(F32)
    w_in = w["w_in"][l]
    lam_init = _lambda_init(l)
    return {
        "attn_norm": row(w["attn_norm"][l]),
        "w_a": w_in[:, :_OFF_GA].astype(BF16),
        "w_ga": jnp.pad(w_in[:, _OFF_GA:_OFF_GR], ((0, 0), (0, LANES - GLA_GATE_RANK))).astype(BF16),
        "w_gr": w_in[:, _OFF_GR:].astype(BF16),
        "dq_norm": row(jnp.tile(w["diff_q_norm"][l], 2 * DIFF_HEADS)) * (DIFF_QK ** -0.5),
        "dk_norm": row(jnp.tile(w["diff_k_norm"][l], 2 * DIFF_HEADS)),
        "bd32": _block_diag(DIFF_W, DIFF_QK, 1.0 / DIFF_QK),
        "w_gate": jnp.pad(w["gla_w_gate"][l], ((0, LANES - GLA_GATE_RANK), (0, 0))).astype(BF16),
        "b_gate": row(w["gla_b_gate"][l]),
        "diff_out_norm": row(jnp.tile(w["diff_out_norm"][l], DIFF_HEADS)) * (1.0 - lam_init),
        "gla_out_norm": row(jnp.tile(w["gla_out_norm"][l], GLA_HEADS)),
        "w_o_d": w["w_o"][l][:DIFF_W].astype(BF16),
        "w_o_g": w["w_o"][l][DIFF_W:].astype(BF16),
        "cross_norm": row(w["cross_norm"][l]),
        "cross_w_q": w["cross_w_q"][l].astype(BF16),
        "cross_q_norm": row(jnp.tile(w["cross_q_norm"][l], CROSS_HEADS)) * (CROSS_HD ** -0.5),
        "mem_norm": row(w["mem_norm"][l]),
        "cross_w_kv": w["cross_w_kv"][l].astype(BF16),
        "cross_k_norm": row(jnp.tile(w["cross_k_norm"][l], CROSS_HEADS)),
        "cross_w_o": w["cross_w_o"][l].astype(BF16),
        "ffn_norm": row(w["ffn_norm"][l]),
        "peer_w_q": w["peer_w_q"][l].astype(BF16),
        "peer_q_norm": row(jnp.tile(w["peer_q_norm"][l], PEER_HEADS)),
        "peer_sub_keys": w["peer_sub_keys"][l].reshape(2 * PEER_HEADS, N_KEYS, N_KEYS).astype(BF16),
        "peer_u": _pack_table(w["peer_u"][l]),
        "peer_v": _pack_table(w["peer_v"][l]),
    }


def _diff_lambda(w, l):
    f = lambda a: a[l].astype(F32)
    lam = (jnp.exp(jnp.sum(f(w["lambda_q1"]) * f(w["lambda_k1"])))
           - jnp.exp(jnp.sum(f(w["lambda_q2"]) * f(w["lambda_k2"]))) + _lambda_init(l))
    return lam.reshape(1, 1)


def _peer(h2, xt, pq, p):
    n = h2.shape[0]
    e, g = _route(pq, p)
    w = _peer_dot(e, xt.reshape(n, 8, LANES), g, p["peer_u"])
    return _peer_sum(e, w, h2.reshape(n, 8, LANES), p["peer_v"]).reshape(n, D_MODEL)


def _tail(od, og, gr, x, mk, mv, p):
    b, t, _ = x.shape
    n = b * t
    flat = lambda a: a.reshape(n, a.shape[-1])
    h1, qc = _mix_out(flat(od), flat(og), flat(gr), flat(x), p)
    h2, xt, pq = _cross(qc.reshape(b, t, D_MODEL), mk, mv, h1.reshape(b, t, D_MODEL), p)
    return _peer(flat(h2), flat(xt), flat(pq), p).reshape(b, t, D_MODEL)


def kernel(x_prompt, x_sample, mem_prompt, cache_diff_k, cache_diff_v, page_table, state_gla,
           cache_mem_k, cache_mem_v, attn_norm, w_in, diff_q_norm, diff_k_norm, lambda_q1,
           lambda_k1, lambda_q2, lambda_k2, diff_out_norm, gla_w_gate, gla_b_gate, gla_out_norm,
           w_o, cross_norm, mem_norm, cross_w_q, cross_w_kv, cross_q_norm, cross_k_norm,
           cross_w_o, ffn_norm, peer_w_q, peer_q_norm, peer_sub_keys, peer_u, peer_v):
    w = dict(attn_norm=attn_norm, w_in=w_in, diff_q_norm=diff_q_norm, diff_k_norm=diff_k_norm,
             lambda_q1=lambda_q1, lambda_k1=lambda_k1, lambda_q2=lambda_q2, lambda_k2=lambda_k2,
             diff_out_norm=diff_out_norm, gla_w_gate=gla_w_gate, gla_b_gate=gla_b_gate,
             gla_out_norm=gla_out_norm, w_o=w_o, cross_norm=cross_norm, mem_norm=mem_norm,
             cross_w_q=cross_w_q, cross_w_kv=cross_w_kv, cross_q_norm=cross_q_norm,
             cross_k_norm=cross_k_norm, cross_w_o=cross_w_o, ffn_norm=ffn_norm,
             peer_w_q=peer_w_q, peer_q_norm=peer_q_norm, peer_sub_keys=peer_sub_keys,
             peer_u=peer_u, peer_v=peer_v)
    b, s, _ = x_prompt.shape
    db, t, _ = x_sample.shape
    depth = w_in.shape[0]
    hp, hs = x_prompt, x_sample
    outs = [[] for _ in range(8)]
    for l in range(depth):
        p = _prep_layer(l, w)
        lam = _diff_lambda(w, l)

        dq, dk, dv, dk16, dv16, gq, gk, gv, la, gr = _proj_in(hp.reshape(b * s, D_MODEL), p)
        seq = lambda a: a.reshape(b, s, a.shape[-1])
        od = _diff_prompt(lam, seq(dq), seq(dk16), seq(dv16), p)
        st0 = jnp.zeros((b, GLA_HEADS // 2, LANES, LANES), F32)
        og, st = _gla(seq(gq), seq(gk), seq(gv), seq(la), st0, p)
        mk, mv = _mem_kv(mem_prompt.reshape(b * N_MEM, D_MODEL), p)
        mk = mk.reshape(b, N_MEM, D_MODEL)
        mv = mv.reshape(b, N_MEM, D_MODEL)
        hp = _tail(od, og, seq(gr), hp, mk, mv, p)
        outs[0].append(dk.reshape(b, s, DIFF_HEADS, DIFF_V))
        outs[1].append(dv.reshape(b, s, DIFF_HEADS, DIFF_V))
        outs[2].append(_pairs_to_state(st))
        outs[3].append(mk.reshape(b, N_MEM, CROSS_HEADS, CROSS_HD))
        outs[4].append(mv.reshape(b, N_MEM, CROSS_HEADS, CROSS_HD))

        dq, dk, dv, dk16, dv16, gq, gk, gv, la, gr = _proj_in(hs.reshape(db * t, D_MODEL), p)
        seq = lambda a: a.reshape(db, t, a.shape[-1])
        od = _diff_sample(lam, page_table, seq(dq), seq(dk16), seq(dv16), cache_diff_k[l],
                          cache_diff_v[l], p)
        tp = -(-t // GLA_SUB) * GLA_SUB
        padt = lambda a: jnp.pad(seq(a), ((0, 0), (0, tp - t), (0, 0)))
        og, st = _gla(padt(gq), padt(gk), padt(gv), padt(la), _state_to_pairs(state_gla[l]), p)
        hs_pad = jnp.pad(hs, ((0, 0), (0, tp - t), (0, 0)))
        y = _tail(jnp.pad(od, ((0, 0), (0, tp - t), (0, 0))).astype(BF16), og, padt(gr), hs_pad,
                  cache_mem_k[l].reshape(db, N_MEM, D_MODEL),
                  cache_mem_v[l].reshape(db, N_MEM, D_MODEL), p)
        hs = y[:, :t]
        outs[5].append(dk.reshape(db, t, DIFF_HEADS, DIFF_V))
        outs[6].append(dv.reshape(db, t, DIFF_HEADS, DIFF_V))
        outs[7].append(_pairs_to_state(st))

    stack = lambda xs: jnp.stack(xs)
    return (hp, hs, stack(outs[0]), stack(outs[1]), stack(outs[2]), stack(outs[3]), stack(outs[4]),
            stack(outs[5]), stack(outs[6]), stack(outs[7]))
```

```python
import functools
import math

import jax
import jax.numpy as jnp
from jax import lax
from jax.experimental import pallas as pl
from jax.experimental.pallas import tpu as pltpu

F32 = jnp.float32
BF16 = jnp.bfloat16
I32 = jnp.int32

EPS = 1e-6
D_MODEL = 1024
DIFF_HEADS = 8
DIFF_QK = 32
DIFF_V = 64
DIFF_W = DIFF_HEADS * DIFF_V
GLA_HEADS = 4
GLA_DK = 64
GLA_DV = 128
GLA_QW = GLA_HEADS * GLA_DK
GLA_VW = GLA_HEADS * GLA_DV
GLA_GATE_RANK = 16
GLA_TAU = 16.0
GLA_SUB = 16
CROSS_HEADS = 4
CROSS_HD = 256
N_MEM = 256
PEER_HEADS = 8
PEER_QD = 256
N_KEYS = 128
PEER_TOPK = 16
PEER_SEL = PEER_HEADS * PEER_TOPK
PAGE_SIZE = 128
LANES = 128
VMEM_LIMIT = 56 * 1024 * 1024

_OFF_GA = 2560
_OFF_GR = 2576


def _params(sem, vmem=None):
    return pltpu.CompilerParams(dimension_semantics=sem, vmem_limit_bytes=vmem)


def _rms(x, g):
    return x * lax.rsqrt(jnp.mean(x * x, axis=-1, keepdims=True) + EPS) * g


def _split_dot(x, w):
    hi = x.astype(BF16)
    lo = (x - hi.astype(F32)).astype(BF16)
    return (jnp.dot(hi, w, preferred_element_type=F32)
            + jnp.dot(lo, w, preferred_element_type=F32))


def _group_rms(x, bd, g):
    ms = _split_dot(x * x, bd)
    return x * lax.rsqrt(ms + EPS) * g


def _block_diag(width, group, value, dtype=BF16):
    i = jnp.arange(width) // group
    return jnp.where(i[:, None] == i[None, :], value, 0.0).astype(dtype)


def _proj_in_kernel(x_ref, g_ref, wa_ref, wgr_ref, wga_ref, qn_ref, kn_ref, bd_ref, wg_ref, bg_ref,
                    dq_ref, dk_ref, dv_ref, dk16_ref, dv16_ref, gq_ref, gk_ref, gv_ref, la_ref,
                    gr_ref):
    xb = _rms(x_ref[...], g_ref[...]).astype(BF16)
    ya = jnp.dot(xb, wa_ref[...], preferred_element_type=F32)
    bd = bd_ref[...]
    dq = _group_rms(ya[:, 0:512], bd, qn_ref[...])
    dk = _group_rms(ya[:, 512:1024], bd, kn_ref[...])
    dv = ya[:, 1024:1536]
    dq_ref[...] = dq.astype(BF16)
    dk_ref[...] = dk
    dv_ref[...] = dv
    dk16_ref[...] = dk.astype(BF16)
    dv16_ref[...] = dv.astype(BF16)
    gq_ref[...] = ya[:, 1536:1792] * (GLA_DK ** -0.5)
    gk_ref[...] = ya[:, 1792:2048]
    gv_ref[...] = ya[:, 2048:2560]
    gr_ref[...] = jnp.dot(xb, wgr_ref[...], preferred_element_type=F32)
    ga = jnp.dot(xb, wga_ref[...], preferred_element_type=F32)
    z = jnp.dot(ga.astype(BF16), wg_ref[...], preferred_element_type=F32) + bg_ref[...]
    log_sig = jnp.minimum(z, 0.0) - jnp.log(1.0 + jnp.exp(-jnp.abs(z)))
    la_ref[...] = log_sig * (1.0 / GLA_TAU)


def _proj_in(x, p, tm=256):
    n = x.shape[0]
    tm = min(tm, n)
    row = lambda w: pl.BlockSpec((tm, w), lambda i: (i, 0))
    full = lambda a: pl.BlockSpec(a.shape, lambda i: (0,) * a.ndim)
    consts = (p["attn_norm"], p["w_a"], p["w_gr"], p["w_ga"], p["dq_norm"], p["dk_norm"],
              p["bd32"], p["w_gate"], p["b_gate"])
    out_w = ((512, BF16), (512, F32), (512, F32), (512, BF16), (512, BF16), (256, F32), (256, F32),
             (512, F32), (256, F32), (512, F32))
    return pl.pallas_call(
        _proj_in_kernel,
        grid=(n // tm,),
        in_specs=[row(D_MODEL)] + [full(a) for a in consts],
        out_specs=[row(w) for w, _ in out_w],
        out_shape=[jax.ShapeDtypeStruct((n, w), dt) for w, dt in out_w],
        compiler_params=_params(("parallel",), VMEM_LIMIT),
        name="proj_in",
    )(x, *consts)


def _mix_out_kernel(od_ref, og_ref, gr_ref, x_ref, gn_ref, wod_ref, wog_ref, cn_ref, wq_ref,
                    qn_ref, h_ref, qc_ref):
    og = og_ref[...]
    gr = gr_ref[...]
    gn = gn_ref[...]
    parts = []
    for h in range(GLA_HEADS):
        sl = slice(h * GLA_DV, (h + 1) * GLA_DV)
        parts.append(_rms(og[:, sl], gn[:, sl]))
    ogn = jnp.concatenate(parts, axis=-1) * (gr * jax.nn.sigmoid(gr))
    mix = (jnp.dot(od_ref[...], wod_ref[...], preferred_element_type=F32)
           + jnp.dot(ogn.astype(BF16), wog_ref[...], preferred_element_type=F32))
    h1 = x_ref[...] + mix
    h_ref[...] = h1
    hn = _rms(h1, cn_ref[...]).astype(BF16)
    q = jnp.dot(hn, wq_ref[...], preferred_element_type=F32)
    qn = qn_ref[...]
    parts = []
    for h in range(CROSS_HEADS):
        sl = slice(h * CROSS_HD, (h + 1) * CROSS_HD)
        parts.append(_rms(q[:, sl], qn[:, sl]))
    qc_ref[...] = jnp.concatenate(parts, axis=-1).astype(BF16)


def _mix_out(od, og, gr, x, p, tm=256):
    n = x.shape[0]
    tm = min(tm, n)
    row = lambda w: pl.BlockSpec((tm, w), lambda i: (i, 0))
    full = lambda a: pl.BlockSpec(a.shape, lambda i: (0,) * a.ndim)
    consts = (p["gla_out_norm"], p["w_o_d"], p["w_o_g"], p["cross_norm"], p["cross_w_q"],
              p["cross_q_norm"])
    return pl.pallas_call(
        _mix_out_kernel,
        grid=(n // tm,),
        in_specs=[row(512), row(512), row(512), row(D_MODEL)] + [full(a) for a in consts],
        out_specs=[row(D_MODEL), row(D_MODEL)],
        out_shape=[jax.ShapeDtypeStruct((n, D_MODEL), F32), jax.ShapeDtypeStruct((n, D_MODEL), BF16)],
        compiler_params=_params(("parallel",), VMEM_LIMIT),
        name="mix_out",
    )(od, og, gr, x, *consts)


def _mem_kv_kernel(m_ref, mn_ref, wkv_ref, kn_ref, k_ref, v_ref):
    mb = _rms(m_ref[...], mn_ref[...]).astype(BF16)
    kv = jnp.dot(mb, wkv_ref[...], preferred_element_type=F32)
    kn = kn_ref[...]
    parts = []
    for h in range(CROSS_HEADS):
        sl = slice(h * CROSS_HD, (h + 1) * CROSS_HD)
        parts.append(_rms(kv[:, sl], kn[:, sl]))
    k_ref[...] = jnp.concatenate(parts, axis=-1)
    v_ref[...] = kv[:, D_MODEL:]


def _mem_kv(mem, p, tm=256):
    n = mem.shape[0]
    tm = min(tm, n)
    row = lambda w: pl.BlockSpec((tm, w), lambda i: (i, 0))
    full = lambda a: pl.BlockSpec(a.shape, lambda i: (0,) * a.ndim)
    consts = (p["mem_norm"], p["cross_w_kv"], p["cross_k_norm"])
    return pl.pallas_call(
        _mem_kv_kernel,
        grid=(n // tm,),
        in_specs=[row(D_MODEL)] + [full(a) for a in consts],
        out_specs=[row(D_MODEL), row(D_MODEL)],
        out_shape=[jax.ShapeDtypeStruct((n, D_MODEL), F32)] * 2,
        compiler_params=_params(("parallel",), VMEM_LIMIT),
        name="mem_kv",
    )(mem, *consts)


def _cross_kernel(qc_ref, mk_ref, mv_ref, h1_ref, wo_ref, fn_ref, wpq_ref, pqn_ref,
                  h2_ref, xt_ref, pq_ref):
    qc = qc_ref[0]
    mk = mk_ref[0].astype(BF16)
    mv = mv_ref[0].astype(BF16)
    parts = []
    for h in range(CROSS_HEADS):
        sl = slice(h * CROSS_HD, (h + 1) * CROSS_HD)
        s = lax.dot_general(qc[:, sl], mk[:, sl], (((1,), (1,)), ((), ())),
                            preferred_element_type=F32)
        e = jnp.exp(s - jnp.max(s, axis=-1, keepdims=True))
        prob = e / jnp.sum(e, axis=-1, keepdims=True)
        parts.append(jnp.dot(prob.astype(BF16), mv[:, sl], preferred_element_type=F32))
    o = jnp.concatenate(parts, axis=-1).astype(BF16)
    h2 = h1_ref[0] + jnp.dot(o, wo_ref[...], preferred_element_type=F32)
    h2_ref[0] = h2
    xt = _rms(h2, fn_ref[...])
    xt_ref[0] = xt
    pq = jnp.dot(xt.astype(BF16), wpq_ref[...], preferred_element_type=F32)
    pqn = pqn_ref[...]
    parts = []
    for h in range(PEER_HEADS):
        sl = slice(h * PEER_QD, (h + 1) * PEER_QD)
        parts.append(_rms(pq[:, sl], pqn[:, sl]))
    pq_ref[0] = jnp.concatenate(parts, axis=-1).astype(BF16)


def _cross(qc, mk, mv, h1, p, tq=256):
    b, t, _ = qc.shape
    tq = min(tq, t)
    row = lambda w: pl.BlockSpec((1, tq, w), lambda i, j: (i, j, 0))
    mem = pl.BlockSpec((1, N_MEM, D_MODEL), lambda i, j: (i, 0, 0))
    full = lambda a: pl.BlockSpec(a.shape, lambda i, j: (0,) * a.ndim)
    consts = (p["cross_w_o"], p["ffn_norm"], p["peer_w_q"], p["peer_q_norm"])
    return pl.pallas_call(
        _cross_kernel,
        grid=(b, t // tq),
        in_specs=[row(D_MODEL), mem, mem, row(D_MODEL)] + [full(a) for a in consts],
        out_specs=[row(D_MODEL), row(D_MODEL), row(2048)],
        out_shape=[jax.ShapeDtypeStruct((b, t, D_MODEL), F32),
                   jax.ShapeDtypeStruct((b, t, D_MODEL), F32),
                   jax.ShapeDtypeStruct((b, t, 2048), BF16)],
        compiler_params=_params(("parallel", "parallel"), VMEM_LIMIT),
        name="cross_attn",
    )(qc, mk, mv, h1, *consts)


def _diff_prompt_kernel(lam_ref, qt_ref, k_ref, vt_ref, on_ref, o_ref, m_scr, l_scr, acc_scr, *, tq):
    qi = pl.program_id(1)
    lam = lam_ref[0, 0]
    cols = 4 * tq
    chan = lax.broadcasted_iota(I32, (LANES, cols), 0)
    my_map = chan // DIFF_QK == lax.broadcasted_iota(I32, (LANES, cols), 1) // tq
    causal = (lax.broadcasted_iota(I32, (tq, cols), 0)
              <= lax.broadcasted_iota(I32, (tq, cols), 1) % tq)
    top_head = lax.broadcasted_iota(I32, (LANES, tq), 0) < DIFF_V
    for p in range(DIFF_HEADS // 2):
        sl = slice(p * LANES, (p + 1) * LANES)
        qtp = qt_ref[0, sl, :]
        qs = jnp.where(my_map, jnp.concatenate([qtp] * 4, axis=1), jnp.zeros((LANES, cols), BF16))
        m_scr[...] = jnp.full((1, cols), -jnp.inf, F32)
        l_scr[...] = jnp.zeros((1, cols), F32)
        acc_scr[...] = jnp.zeros((LANES, cols), F32)

        def block(kb, masked, qs=qs):
            t0 = pl.multiple_of(kb * tq, tq)
            s = jnp.dot(k_ref[0, pl.ds(t0, tq), sl], qs, preferred_element_type=F32)
            if masked:
                s = jnp.where(causal, s, -jnp.inf)
            m_old = m_scr[...]
            m_new = jnp.maximum(m_old, jnp.max(s, axis=0, keepdims=True))
            alpha = jnp.exp(m_old - m_new)
            e = jnp.exp(s - m_new)
            m_scr[...] = m_new
            l_scr[...] = alpha * l_scr[...] + jnp.sum(e, axis=0, keepdims=True)
            acc_scr[...] = alpha * acc_scr[...] + jnp.dot(
                vt_ref[0, sl, pl.ds(t0, tq)], e.astype(BF16), preferred_element_type=F32)

        def body(kb, carry):
            block(kb, False)
            return carry

        lax.fori_loop(0, qi, body, 0)
        block(qi, True)
        on = acc_scr[...] / l_scr[...]
        ot = jnp.where(top_head, on[:, :tq] - lam * on[:, tq:2 * tq],
                       on[:, 2 * tq:3 * tq] - lam * on[:, 3 * tq:])
        sq = ot * ot
        ms = jnp.where(top_head, jnp.sum(sq[:DIFF_V], axis=0, keepdims=True),
                       jnp.sum(sq[DIFF_V:], axis=0, keepdims=True)) * (1.0 / DIFF_V)
        ot = ot * lax.rsqrt(ms + EPS) * on_ref[sl, :]
        o_ref[0, :, sl] = ot.T.astype(BF16)


def _diff_prompt(lam, dq, dk16, dv16, p, tq=256):
    b, s, _ = dq.shape
    tq = min(tq, s)
    qt = jnp.swapaxes(dq, 1, 2)
    vt = jnp.swapaxes(dv16, 1, 2)
    return pl.pallas_call(
        functools.partial(_diff_prompt_kernel, tq=tq),
        grid=(b, s // tq),
        in_specs=[pl.BlockSpec(memory_space=pltpu.SMEM),
                  pl.BlockSpec((1, DIFF_W, tq), lambda i, j: (i, 0, j)),
                  pl.BlockSpec((1, s, DIFF_W), lambda i, j: (i, 0, 0)),
                  pl.BlockSpec((1, DIFF_W, s), lambda i, j: (i, 0, 0)),
                  pl.BlockSpec((DIFF_W, 1), lambda i, j: (0, 0))],
        out_specs=pl.BlockSpec((1, tq, DIFF_W), lambda i, j: (i, j, 0)),
        out_shape=jax.ShapeDtypeStruct((b, s, DIFF_W), BF16),
        scratch_shapes=[pltpu.VMEM((1, 4 * tq), F32), pltpu.VMEM((1, 4 * tq), F32),
                        pltpu.VMEM((LANES, 4 * tq), F32)],
        compiler_params=_params(("parallel", "parallel"), VMEM_LIMIT),
        name="diff_prompt",
    )(lam, qt, dk16, vt, p["diff_out_norm"].reshape(DIFF_W, 1))


def _gla_kernel(q_ref, k_ref, v_ref, la_ref, st0_ref, tri_ref, ones_ref, o_ref, stout_ref,
                st_scr, b_scr, *, ts):
    si = pl.program_id(1)

    @pl.when(si == 0)
    def _():
        st_scr[...] = st0_ref[0]

    la = la_ref[0]
    hi = la.astype(BF16)
    r1 = la - hi.astype(F32)
    mid = r1.astype(BF16)
    lo = (r1 - mid.astype(F32)).astype(BF16)
    tri = tri_ref[...]
    b = (jnp.dot(tri, hi, preferred_element_type=F32) + jnp.dot(tri, mid, preferred_element_type=F32)
         + jnp.dot(tri, lo, preferred_element_type=F32))
    b_scr[...] = b

    q = q_ref[0]
    k = k_ref[0]
    v = v_ref[0]
    sub_q = lax.broadcasted_iota(I32, (ts, GLA_QW), 0) % GLA_SUB
    sub_v = lax.broadcasted_iota(I32, (ts, GLA_VW), 0) % GLA_SUB
    ones = ones_ref[...]
    o_intra = jnp.zeros((ts, GLA_VW), F32)
    for d in range(GLA_SUB):
        if d == 0:
            kd, bd, vd = k, b, v
        else:
            kd = pltpu.roll(k, d, 0)
            bd = pltpu.roll(b, d, 0)
            vd = pltpu.roll(v, d, 0)
        ok = sub_q >= d
        e = jnp.where(ok, q * kd * jnp.exp(jnp.where(ok, b - bd, 0.0)), 0.0)
        w = jnp.dot(e.astype(BF16), ones, preferred_element_type=F32)
        o_intra = o_intra + w * jnp.where(sub_v >= d, vd, 0.0)
    o_ref[0] = o_intra

    lane = lax.broadcasted_iota(I32, (GLA_SUB, LANES), 1)
    lane_sq = lax.broadcasted_iota(I32, (LANES, LANES), 1)

    def step(c, carry):
        t0 = pl.multiple_of(c * GLA_SUB, GLA_SUB)
        bc = b_scr[pl.ds(t0, GLA_SUB), :]
        qc = q_ref[0, pl.ds(t0, GLA_SUB), :]
        kc = k_ref[0, pl.ds(t0, GLA_SUB), :]
        vc = v_ref[0, pl.ds(t0, GLA_SUB), :]
        bl = bc[GLA_SUB - 1:GLA_SUB, :]
        qe = qc * jnp.exp(bc)
        ke = (kc * jnp.exp(bl - bc)).astype(BF16)
        dec = jnp.exp(bl)
        o_parts = []
        for p in range(GLA_HEADS // 2):
            sl = slice(p * LANES, (p + 1) * LANES)
            st = st_scr[p]
            stb = st.astype(BF16)
            qp = qe[:, sl]
            upd = []
            for hh in range(2):
                mine = (lane < GLA_DK) if hh == 0 else (lane >= GLA_DK)
                qm = jnp.where(mine, qp, 0.0).astype(BF16)
                o_parts.append(lax.dot_general(qm, stb, (((1,), (1,)), ((), ())),
                                               preferred_element_type=F32))
                vh = vc[:, (2 * p + hh) * GLA_DV:(2 * p + hh + 1) * GLA_DV]
                upd.append(jnp.dot(vh.T.astype(BF16), ke[:, sl], preferred_element_type=F32))
            st_scr[p] = dec[:, sl] * st + jnp.where(lane_sq < GLA_DK, upd[0], upd[1])
        o_ref[0, pl.ds(t0, GLA_SUB), :] += jnp.concatenate(o_parts, axis=-1)
        return carry

    lax.fori_loop(0, ts // GLA_SUB, step, 0)

    @pl.when(si == pl.num_programs(1) - 1)
    def _():
        stout_ref[0] = st_scr[...]


def _gla(gq, gk, gv, la, st0, p, ts=256):
    b, t, _ = gq.shape
    ts = min(ts, t)
    i = jnp.arange(ts)
    tri = ((i[:, None] // GLA_SUB == i[None, :] // GLA_SUB) & (i[None, :] <= i[:, None])).astype(BF16)
    ones = _block_diag_rect(GLA_QW, GLA_DK, GLA_VW, GLA_DV)
    seq = lambda w: pl.BlockSpec((1, ts, w), lambda bi, si: (bi, si, 0))
    stspec = pl.BlockSpec((1, GLA_HEADS // 2, LANES, LANES), lambda bi, si: (bi, 0, 0, 0))
    return pl.pallas_call(
        functools.partial(_gla_kernel, ts=ts),
        grid=(b, t // ts),
        in_specs=[seq(GLA_QW), seq(GLA_QW), seq(GLA_VW), seq(GLA_QW), stspec,
                  pl.BlockSpec((ts, ts), lambda bi, si: (0, 0)),
                  pl.BlockSpec((GLA_QW, GLA_VW), lambda bi, si: (0, 0))],
        out_specs=[seq(GLA_VW), stspec],
        out_shape=[jax.ShapeDtypeStruct((b, t, GLA_VW), F32),
                   jax.ShapeDtypeStruct((b, GLA_HEADS // 2, LANES, LANES), F32)],
        scratch_shapes=[pltpu.VMEM((GLA_HEADS // 2, LANES, LANES), F32),
                        pltpu.VMEM((ts, GLA_QW), F32)],
        compiler_params=_params(("parallel", "arbitrary"), VMEM_LIMIT),
        name="gla",
    )(gq, gk, gv, la, st0, tri, ones)


PAGES_PER_STEP = 8
NEW_T = 8


def _diff_sample_kernel(pt_ref, lam_ref, q_ref, *rest):
    del pt_ref
    np_ = PAGES_PER_STEP
    k_refs = rest[:np_]
    v_refs = rest[np_:2 * np_]
    kn_ref, vn_ref, on_ref, o_ref, m_scr, l_scr, acc_scr = rest[2 * np_:]
    i = pl.program_id(1)

    @pl.when(i == 0)
    def _():
        m_scr[...] = jnp.full(m_scr.shape, -jnp.inf, F32)
        l_scr[...] = jnp.zeros(l_scr.shape, F32)
        acc_scr[...] = jnp.zeros(acc_scr.shape, F32)

    q = q_ref[0]

    def update(scores, values):
        m_old = m_scr[...]
        m_new = m_old
        for s in scores:
            m_new = jnp.maximum(m_new, jnp.max(s, axis=-1, keepdims=True))
        alpha = jnp.exp(m_old - m_new)
        l = alpha * l_scr[...]
        acc = alpha * acc_scr[...]
        for s, v in zip(scores, values):
            e = jnp.exp(s - m_new)
            l = l + jnp.sum(e, axis=-1, keepdims=True)
            acc = acc + jnp.dot(e.astype(BF16), v, preferred_element_type=F32)
        m_scr[...] = m_new
        l_scr[...] = l
        acc_scr[...] = acc

    nt = (((1,), (1,)), ((), ()))
    scores = [lax.dot_general(q, k_ref[0].astype(BF16), nt, preferred_element_type=F32)
              for k_ref in k_refs]
    update(scores, [v_ref[0].astype(BF16) for v_ref in v_refs])

    @pl.when(i == pl.num_programs(1) - 1)
    def _():
        s = lax.dot_general(q, kn_ref[0], nt, preferred_element_type=F32)
        t_q = lax.broadcasted_iota(I32, s.shape, 0) % NEW_T
        t_k = lax.broadcasted_iota(I32, s.shape, 1)
        update([jnp.where(t_k <= t_q, s, -jnp.inf)], [vn_ref[0]])
        accn = acc_scr[...] / l_scr[...]
        rows = 2 * NEW_T
        lane = lax.broadcasted_iota(I32, (rows, DIFF_W), 1)
        o16 = jnp.zeros((rows, DIFF_W), F32)
        for h in range(DIFF_HEADS):
            o16 = jnp.where(lane // DIFF_V == h, accn[h * rows:(h + 1) * rows, :], o16)
        o = o16[:NEW_T] - lam_ref[0, 0] * o16[NEW_T:]
        lane = lane[:NEW_T]
        sq = o * o
        ms = jnp.zeros_like(o)
        for h in range(DIFF_HEADS):
            mine = lane // DIFF_V == h
            ms = jnp.where(mine, jnp.sum(jnp.where(mine, sq, 0.0), axis=-1, keepdims=True), ms)
        o_ref[0] = o * lax.rsqrt(ms * (1.0 / DIFF_V) + EPS) * on_ref[...]


def _diff_sample(lam, page_table, dq, dk16, dv16, cache_k, cache_v, p):
    db, t, _ = dq.shape
    n_pages = page_table.shape[1]
    n_phys = cache_k.shape[0]
    np_ = PAGES_PER_STEP
    a = dq.reshape(db, t, 16, DIFF_QK).transpose(0, 2, 1, 3)
    q_bd = (a[:, :, :, None, :] * jnp.eye(16, dtype=BF16)[None, :, None, :, None]
            ).reshape(db, 16 * t, DIFF_W)
    pad = lambda x: jnp.pad(x, ((0, 0), (0, LANES - t), (0, 0)))
    ck = cache_k.reshape(n_phys, PAGE_SIZE, DIFF_W)
    cv = cache_v.reshape(n_phys, PAGE_SIZE, DIFF_W)

    def page_spec(n):
        return pl.BlockSpec((1, PAGE_SIZE, DIFF_W), lambda b, i, pt: (pt[b, i * np_ + n], 0, 0))

    per_b = lambda r: pl.BlockSpec((1, r, DIFF_W), lambda b, i, pt: (b, 0, 0))
    grid_spec = pltpu.PrefetchScalarGridSpec(
        num_scalar_prefetch=1,
        grid=(db, n_pages // np_),
        in_specs=[pl.BlockSpec(memory_space=pltpu.SMEM), per_b(16 * t)]
        + [page_spec(n) for n in range(np_)] * 2
        + [per_b(LANES), per_b(LANES), pl.BlockSpec((1, DIFF_W), lambda b, i, pt: (0, 0))],
        out_specs=per_b(t),
        scratch_shapes=[pltpu.VMEM((16 * t, 1), F32), pltpu.VMEM((16 * t, 1), F32),
                        pltpu.VMEM((16 * t, DIFF_W), F32)],
    )
    return pl.pallas_call(
        _diff_sample_kernel,
        grid_spec=grid_spec,
        out_shape=jax.ShapeDtypeStruct((db, t, DIFF_W), F32),
        compiler_params=_params(("parallel", "arbitrary"), VMEM_LIMIT),
        name="diff_sample",
    )(page_table, lam, q_bd, *([ck] * np_), *([cv] * np_), pad(dk16), pad(dv16),
      p["diff_out_norm"])


def _top16_rows(s, row):
    vals, idxs = [], []
    sentinel = s.shape[0]
    for _ in range(PEER_TOPK):
        m = jnp.max(s, axis=0, keepdims=True)
        idx = jnp.min(jnp.where(s == m, row, sentinel), axis=0, keepdims=True)
        vals.append(m)
        idxs.append(idx)
        s = jnp.where(row == idx, -jnp.inf, s)
    return jnp.concatenate(vals, axis=0), jnp.concatenate(idxs, axis=0)


def _pick_row(table, row, idx):
    return jnp.max(jnp.where(row == idx, table, -1), axis=0, keepdims=True)


_CAND_WIDTH = [PEER_TOPK // (a + 1) for a in range(PEER_TOPK)]
_CAND_ROWS = -(-sum(_CAND_WIDTH) // 8) * 8


def _route_kernel(pq_ref, sk_ref, e_ref, g_ref, *, tb):
    nt = (((1,), (1,)), ((), ()))
    row_k = lax.broadcasted_iota(I32, (N_KEYS, tb), 0)
    row_c = lax.broadcasted_iota(I32, (_CAND_ROWS, tb), 0)
    n_pad = _CAND_ROWS - sum(_CAND_WIDTH)
    e_rows, g_rows = [], []
    for h in range(PEER_HEADS):
        tops = []
        for c in range(2):
            hc = 2 * h + c
            s = lax.dot_general(sk_ref[hc], pq_ref[:, hc * N_KEYS:(hc + 1) * N_KEYS], nt,
                                preferred_element_type=F32)
            tops.append(_top16_rows(s, row_k))
        (s1, i1), (s2, i2) = tops
        cand = jnp.concatenate(
            [jnp.broadcast_to(s1[a:a + 1], (nb, tb)) + s2[:nb] for a, nb in enumerate(_CAND_WIDTH)]
            + [jnp.full((n_pad, tb), -jnp.inf, F32)], axis=0)
        cand_e = jnp.concatenate(
            [jnp.broadcast_to(i1[a:a + 1], (nb, tb)) * N_KEYS + i2[:nb]
             for a, nb in enumerate(_CAND_WIDTH)] + [jnp.zeros((n_pad, tb), I32)], axis=0)
        top_s, j = _top16_rows(cand, row_c)
        e_rows.append(jnp.concatenate(
            [_pick_row(cand_e, row_c, j[r:r + 1]) for r in range(PEER_TOPK)], axis=0))
        ex = jnp.exp(top_s - top_s[0:1])
        g_rows.append(ex / jnp.sum(ex, axis=0, keepdims=True))
    e_ref[...] = jnp.concatenate(e_rows, axis=0).T
    g_ref[...] = jnp.concatenate(g_rows, axis=0).T


def _route(pq, p, tb=128):
    n = pq.shape[0]
    tb = min(tb, n)
    return pl.pallas_call(
        functools.partial(_route_kernel, tb=tb),
        grid=(n // tb,),
        in_specs=[pl.BlockSpec((tb, 2048), lambda i: (i, 0)),
                  pl.BlockSpec((2 * PEER_HEADS, N_KEYS, N_KEYS), lambda i: (0, 0, 0))],
        out_specs=[pl.BlockSpec((tb, PEER_SEL), lambda i: (i, 0))] * 2,
        out_shape=[jax.ShapeDtypeStruct((n, PEER_SEL), I32), jax.ShapeDtypeStruct((n, PEER_SEL), F32)],
        compiler_params=_params(("parallel",), VMEM_LIMIT),
        name="peer_route",
    )(pq, p["peer_sub_keys"])


HALF_ROWS = 4


def _unpack_pair(row):
    lo = lax.bitcast_convert_type(row << 16, F32)
    hi = lax.bitcast_convert_type(row & jnp.uint32(0xFFFF0000), F32)
    return lo, hi


def _gelu_tanh(x):
    return 0.5 * x * (1.0 + jnp.tanh(math.sqrt(2.0 / math.pi) * (x + 0.044715 * (x * x * x))))


def _peer_dot_kernel(e_ref, xt_ref, g_ref, tab_ref, gsum_ref, w_ref, prod_scr, *, tb):
    lane = lax.broadcasted_iota(I32, (PEER_SEL, tb), 1)
    gsum = gsum_ref[...]

    def gather(t):
        xl = xt_ref[t, :HALF_ROWS, :]
        xh = xt_ref[t, HALF_ROWS:, :]
        parts = []
        for j in range(PEER_SEL):
            lo, hi = _unpack_pair(tab_ref[e_ref[t, j]])
            parts.append(lo * xl + hi * xh)
        prod_scr[t % 4] = jnp.concatenate(parts, axis=0).astype(BF16)

    def group_sum(t):
        return jnp.dot(gsum, prod_scr[t % 4], preferred_element_type=F32)

    def lane_sum(t, per_lane, ht):
        dots = jnp.sum(per_lane, axis=1, keepdims=True)
        return jnp.where(lane == t, dots, ht)

    def token(t, carry):
        per_lane, ht = carry
        ht = lane_sum(t - 2, per_lane, ht)
        per_lane = group_sum(t - 1)
        gather(t)
        return per_lane, ht

    gather(0)
    per_lane = group_sum(0)
    gather(1)
    per_lane, ht = lax.fori_loop(2, tb, token, (per_lane, jnp.zeros((PEER_SEL, tb), F32)))
    ht = lane_sum(tb - 2, per_lane, ht)
    ht = lane_sum(tb - 1, group_sum(tb - 1), ht)
    w_ref[...] = g_ref[...] * _gelu_tanh(ht.T)


def _peer_dot(e, xt3, g, tab, tb=128):
    n = e.shape[0]
    tb = min(tb, n)
    gsum = _block_diag_rect(PEER_SEL, 1, PEER_SEL * HALF_ROWS, HALF_ROWS)
    return pl.pallas_call(
        functools.partial(_peer_dot_kernel, tb=tb),
        grid=(n // tb,),
        in_specs=[pl.BlockSpec((tb, PEER_SEL), lambda i: (i, 0), memory_space=pltpu.SMEM),
                  pl.BlockSpec((tb, 8, LANES), lambda i: (i, 0, 0)),
                  pl.BlockSpec((tb, PEER_SEL), lambda i: (i, 0)),
                  pl.BlockSpec(memory_space=pltpu.VMEM),
                  pl.BlockSpec(gsum.shape, lambda i: (0, 0))],
        out_specs=pl.BlockSpec((tb, PEER_SEL), lambda i: (i, 0)),
        out_shape=jax.ShapeDtypeStruct((n, PEER_SEL), F32),
        scratch_shapes=[pltpu.VMEM((4, PEER_SEL * HALF_ROWS, LANES), BF16)],
        compiler_params=_params(("arbitrary",), VMEM_LIMIT),
        name="peer_dot",
    )(e, xt3, g, tab, gsum)


def _peer_sum_kernel(e_ref, w_ref, h_ref, tab_ref, y_ref, *, tb):
    eye = (lax.broadcasted_iota(I32, (PEER_SEL, PEER_SEL), 0)
           == lax.broadcasted_iota(I32, (PEER_SEL, PEER_SEL), 1))

    def weight_column(t):
        wrow = w_ref[pl.ds(t, 1), :]
        return jnp.sum(jnp.where(eye, wrow, 0.0), axis=1, keepdims=True)

    def token(t, wcol):
        wnext = weight_column(jnp.minimum(t + 1, tb - 1))
        acc_lo = jnp.zeros((HALF_ROWS, LANES), F32)
        acc_hi = jnp.zeros((HALF_ROWS, LANES), F32)
        for j in range(PEER_SEL):
            lo, hi = _unpack_pair(tab_ref[e_ref[t, j]])
            w = wcol[j:j + 1, :]
            acc_lo = acc_lo + w * lo
            acc_hi = acc_hi + w * hi
        y_ref[t] = h_ref[t] + jnp.concatenate([acc_lo, acc_hi], axis=0)
        return wnext

    lax.fori_loop(0, tb, token, weight_column(0))


def _peer_sum(e, w, h3, tab, tb=128):
    n = e.shape[0]
    tb = min(tb, n)
    tok = lambda: pl.BlockSpec((tb, 8, LANES), lambda i: (i, 0, 0))
    return pl.pallas_call(
        functools.partial(_peer_sum_kernel, tb=tb),
        grid=(n // tb,),
        in_specs=[pl.BlockSpec((tb, PEER_SEL), lambda i: (i, 0), memory_space=pltpu.SMEM),
                  pl.BlockSpec((tb, PEER_SEL), lambda i: (i, 0)), tok(),
                  pl.BlockSpec(memory_space=pltpu.VMEM)],
        out_specs=tok(),
        out_shape=jax.ShapeDtypeStruct((n, 8, LANES), F32),
        compiler_params=_params(("arbitrary",), VMEM_LIMIT),
        name="peer_sum",
    )(e, w, h3, tab)


def _pack_table(w):
    bits = lax.bitcast_convert_type(w.astype(BF16), jnp.uint16).astype(jnp.uint32)
    half = D_MODEL // 2
    return (bits[:, :half] | (bits[:, half:] << 16)).reshape(w.shape[0], HALF_ROWS, LANES)


def _block_diag_rect(rows, rgroup, cols, cgroup):
    r = jnp.arange(rows) // rgroup
    c = jnp.arange(cols) // cgroup
    return (r[:, None] == c[None, :]).astype(BF16)


def _state_to_pairs(s):
    b = s.shape[0]
    return s.reshape(b, 2, 2, GLA_DK, GLA_DV).transpose(0, 1, 4, 2, 3).reshape(b, 2, LANES, LANES)


def _pairs_to_state(st):
    b = st.shape[0]
    return st.reshape(b, 2, GLA_DV, 2, GLA_DK).transpose(0, 1, 3, 4, 2).reshape(
        b, GLA_HEADS, GLA_DK, GLA_DV)


def _lambda_init(layer):
    return 0.8 - 0.6 * math.exp(-0.3 * layer)


def _prep_layer(l, w):
    row = lambda a: a.reshape(1, -1).astype(F32)
    w_in = w["w_in"][l]
    lam_init = _lambda_init(l)
    return {
        "attn_norm": row(w["attn_norm"][l]),
        "w_a": w_in[:, :_OFF_GA].astype(BF16),
        "w_ga": jnp.pad(w_in[:, _OFF_GA:_OFF_GR], ((0, 0), (0, LANES - GLA_GATE_RANK))).astype(BF16),
        "w_gr": w_in[:, _OFF_GR:].astype(BF16),
        "dq_norm": row(jnp.tile(w["diff_q_norm"][l], 2 * DIFF_HEADS)) * (DIFF_QK ** -0.5),
        "dk_norm": row(jnp.tile(w["diff_k_norm"][l], 2 * DIFF_HEADS)),
        "bd32": _block_diag(DIFF_W, DIFF_QK, 1.0 / DIFF_QK),
        "w_gate": jnp.pad(w["gla_w_gate"][l], ((0, LANES - GLA_GATE_RANK), (0, 0))).astype(BF16),
        "b_gate": row(w["gla_b_gate"][l]),
        "diff_out_norm": row(jnp.tile(w["diff_out_norm"][l], DIFF_HEADS)) * (1.0 - lam_init),
        "gla_out_norm": row(jnp.tile(w["gla_out_norm"][l], GLA_HEADS)),
        "w_o_d": w["w_o"][l][:DIFF_W].astype(BF16),
        "w_o_g": w["w_o"][l][DIFF_W:].astype(BF16),
        "cross_norm": row(w["cross_norm"][l]),
        "cross_w_q": w["cross_w_q"][l].astype(BF16),
        "cross_q_norm": row(jnp.tile(w["cross_q_norm"][l], CROSS_HEADS)) * (CROSS_HD ** -0.5),
        "mem_norm": row(w["mem_norm"][l]),
        "cross_w_kv": w["cross_w_kv"][l].astype(BF16),
        "cross_k_norm": row(jnp.tile(w["cross_k_norm"][l], CROSS_HEADS)),
        "cross_w_o": w["cross_w_o"][l].astype(BF16),
        "ffn_norm": row(w["ffn_norm"][l]),
        "peer_w_q": w["peer_w_q"][l].astype(BF16),
        "peer_q_norm": row(jnp.tile(w["peer_q_norm"][l], PEER_HEADS)),
        "peer_sub_keys": w["peer_sub_keys"][l].reshape(2 * PEER_HEADS, N_KEYS, N_KEYS).astype(BF16),
        "peer_u": _pack_table(w["peer_u"][l]),
        "peer_v": _pack_table(w["peer_v"][l]),
    }


def _diff_lambda(w, l):
    f = lambda a: a[l].astype(F32)
    lam = (jnp.exp(jnp.sum(f(w["lambda_q1"]) * f(w["lambda_k1"])))
           - jnp.exp(jnp.sum(f(w["lambda_q2"]) * f(w["lambda_k2"]))) + _lambda_init(l))
    return lam.reshape(1, 1)


def _peer(h2, xt, pq, p):
    n = h2.shape[0]
    e, g = _route(pq, p)
    w = _peer_dot(e, xt.reshape(n, 8, LANES), g, p["peer_u"])
    return _peer_sum(e, w, h2.reshape(n, 8, LANES), p["peer_v"]).reshape(n, D_MODEL)


def _tail(od, og, gr, x, mk, mv, p):
    b, t, _ = x.shape
    n = b * t
    flat = lambda a: a.reshape(n, a.shape[-1])
    h1, qc = _mix_out(flat(od), flat(og), flat(gr), flat(x), p)
    h2, xt, pq = _cross(qc.reshape(b, t, D_MODEL), mk, mv, h1.reshape(b, t, D_MODEL), p)
    return _peer(flat(h2), flat(xt), flat(pq), p).reshape(b, t, D_MODEL)


def kernel(x_prompt, x_sample, mem_prompt, cache_diff_k, cache_diff_v, page_table, state_gla,
           cache_mem_k, cache_mem_v, attn_norm, w_in, diff_q_norm, diff_k_norm, lambda_q1,
           lambda_k1, lambda_q2, lambda_k2, diff_out_norm, gla_w_gate, gla_b_gate, gla_out_norm,
           w_o, cross_norm, mem_norm, cross_w_q, cross_w_kv, cross_q_norm, cross_k_norm,
           cross_w_o, ffn_norm, peer_w_q, peer_q_norm, peer_sub_keys, peer_u, peer_v):
    w = dict(attn_norm=attn_norm, w_in=w_in, diff_q_norm=diff_q_norm, diff_k_norm=diff_k_norm,
             lambda_q1=lambda_q1, lambda_k1=lambda_k1, lambda_q2=lambda_q2, lambda_k2=lambda_k2,
             diff_out_norm=diff_out_norm, gla_w_gate=gla_w_gate, gla_b_gate=gla_b_gate,
             gla_out_norm=gla_out_norm, w_o=w_o, cross_norm=cross_norm, mem_norm=mem_norm,
             cross_w_q=cross_w_q, cross_w_kv=cross_w_kv, cross_q_norm=cross_q_norm,
             cross_k_norm=cross_k_norm, cross_w_o=cross_w_o, ffn_norm=ffn_norm,
             peer_w_q=peer_w_q, peer_q_norm=peer_q_norm, peer_sub_keys=peer_sub_keys,
             peer_u=peer_u, peer_v=peer_v)
    b, s, _ = x_prompt.shape
    db, t, _ = x_sample.shape
    depth = w_in.shape[0]
    hp, hs = x_prompt, x_sample
    outs = [[] for _ in range(8)]
    for l in range(depth):
        p = _prep_layer(l, w)
        lam = _diff_lambda(w, l)

        dq, dk, dv, dk16, dv16, gq, gk, gv, la, gr = _proj_in(hp.reshape(b * s, D_MODEL), p)
        seq = lambda a: a.reshape(b, s, a.shape[-1])
        od = _diff_prompt(lam, seq(dq), seq(dk16), seq(dv16), p)
        st0 = jnp.zeros((b, GLA_HEADS // 2, LANES, LANES), F32)
        og, st = _gla(seq(gq), seq(gk), seq(gv), seq(la), st0, p)
        mk, mv = _mem_kv(mem_prompt.reshape(b * N_MEM, D_MODEL), p)
        mk = mk.reshape(b, N_MEM, D_MODEL)
        mv = mv.reshape(b, N_MEM, D_MODEL)
        hp = _tail(od, og, seq(gr), hp, mk, mv, p)
        outs[0].append(dk.reshape(b, s, DIFF_HEADS, DIFF_V))
        outs[1].append(dv.reshape(b, s, DIFF_HEADS, DIFF_V))
        outs[2].append(_pairs_to_state(st))
        outs[3].append(mk.reshape(b, N_MEM, CROSS_HEADS, CROSS_HD))
        outs[4].append(mv.reshape(b, N_MEM, CROSS_HEADS, CROSS_HD))

        dq, dk, dv, dk16, dv16, gq, gk, gv, la, gr = _proj_in(hs.reshape(db * t, D_MODEL), p)
        seq = lambda a: a.reshape(db, t, a.shape[-1])
        od = _diff_sample(lam, page_table, seq(dq), seq(dk16), seq(dv16), cache_diff_k[l],
                          cache_diff_v[l], p)
        tp = -(-t // GLA_SUB) * GLA_SUB
        padt = lambda a: jnp.pad(seq(a), ((0, 0), (0, tp - t), (0, 0)))
        og, st = _gla(padt(gq), padt(gk), padt(gv), padt(la), _state_to_pairs(state_gla[l]), p)
        hs_pad = jnp.pad(hs, ((0, 0), (0, tp - t), (0, 0)))
        y = _tail(jnp.pad(od, ((0, 0), (0, tp - t), (0, 0))).astype(BF16), og, padt(gr), hs_pad,
                  cache_mem_k[l].reshape(db, N_MEM, D_MODEL),
                  cache_mem_v[l].reshape(db, N_MEM, D_MODEL), p)
        hs = y[:, :t]
        outs[5].append(dk.reshape(db, t, DIFF_HEADS, DIFF_V))
        outs[6].append(dv.reshape(db, t, DIFF_HEADS, DIFF_V))
        outs[7].append(_pairs_to_state(st))

    stack = lambda xs: jnp.stack(xs)
    return (hp, hs, stack(outs[0]), stack(outs[1]), stack(outs[2]), stack(outs[3]), stack(outs[4]),
            stack(outs[5]), stack(outs[6]), stack(outs[7]))
```

```python
import functools
import math

import jax
import jax.numpy as jnp
from jax import lax
from jax.experimental import pallas as pl
from jax.experimental.pallas import tpu as pltpu

F32 = jnp.float32
BF16 = jnp.bfloat16
I32 = jnp.int32

EPS = 1e-6
D_MODEL = 1024
DIFF_HEADS = 8
DIFF_QK = 32
DIFF_V = 64
DIFF_W = DIFF_HEADS * DIFF_V
GLA_HEADS = 4
GLA_DK = 64
GLA_DV = 128
GLA_QW = GLA_HEADS * GLA_DK
GLA_VW = GLA_HEADS * GLA_DV
GLA_GATE_RANK = 16
GLA_TAU = 16.0
GLA_SUB = 16
CROSS_HEADS = 4
CROSS_HD = 256
N_MEM = 256
PEER_HEADS = 8
PEER_QD = 256
N_KEYS = 128
PEER_TOPK = 16
PEER_SEL = PEER_HEADS * PEER_TOPK
PAGE_SIZE = 128
LANES = 128
VMEM_LIMIT = 56 * 1024 * 1024

_OFF_GA = 2560
_OFF_GR = 2576


def _params(sem, vmem=None):
    return pltpu.CompilerParams(dimension_semantics=sem, vmem_limit_bytes=vmem)


def _rms(x, g):
    return x * lax.rsqrt(jnp.mean(x * x, axis=-1, keepdims=True) + EPS) * g


def _split_dot(x, w):
    hi = x.astype(BF16)
    lo = (x - hi.astype(F32)).astype(BF16)
    return (jnp.dot(hi, w, preferred_element_type=F32)
            + jnp.dot(lo, w, preferred_element_type=F32))


def _group_rms(x, bd, g):
    ms = _split_dot(x * x, bd)
    return x * lax.rsqrt(ms + EPS) * g


def _block_diag(width, group, value, dtype=BF16):
    i = jnp.arange(width) // group
    return jnp.where(i[:, None] == i[None, :], value, 0.0).astype(dtype)


def _proj_in_kernel(x_ref, g_ref, wa_ref, wgr_ref, wga_ref, qn_ref, kn_ref, bd_ref, wg_ref, bg_ref,
                    dq_ref, dk_ref, dv_ref, dk16_ref, dv16_ref, gq_ref, gk_ref, gv_ref, la_ref,
                    gr_ref):
    xb = _rms(x_ref[...], g_ref[...]).astype(BF16)
    ya = jnp.dot(xb, wa_ref[...], preferred_element_type=F32)
    bd = bd_ref[...]
    dq = _group_rms(ya[:, 0:512], bd, qn_ref[...])
    dk = _group_rms(ya[:, 512:1024], bd, kn_ref[...])
    dv = ya[:, 1024:1536]
    dq_ref[...] = dq.astype(BF16)
    dk_ref[...] = dk
    dv_ref[...] = dv
    dk16_ref[...] = dk.astype(BF16)
    dv16_ref[...] = dv.astype(BF16)
    gq_ref[...] = ya[:, 1536:1792] * (GLA_DK ** -0.5)
    gk_ref[...] = ya[:, 1792:2048]
    gv_ref[...] = ya[:, 2048:2560]
    gr_ref[...] = jnp.dot(xb, wgr_ref[...], preferred_element_type=F32)
    ga = jnp.dot(xb, wga_ref[...], preferred_element_type=F32)
    z = jnp.dot(ga.astype(BF16), wg_ref[...], preferred_element_type=F32) + bg_ref[...]
    log_sig = jnp.minimum(z, 0.0) - jnp.log(1.0 + jnp.exp(-jnp.abs(z)))
    la_ref[...] = log_sig * (1.0 / GLA_TAU)


def _proj_in(x, p, tm=256):
    n = x.shape[0]
    tm = min(tm, n)
    row = lambda w: pl.BlockSpec((tm, w), lambda i: (i, 0))
    full = lambda a: pl.BlockSpec(a.shape, lambda i: (0,) * a.ndim)
    consts = (p["attn_norm"], p["w_a"], p["w_gr"], p["w_ga"], p["dq_norm"], p["dk_norm"],
              p["bd32"], p["w_gate"], p["b_gate"])
    out_w = ((512, BF16), (512, F32), (512, F32), (512, BF16), (512, BF16), (256, F32), (256, F32),
             (512, F32), (256, F32), (512, F32))
    return pl.pallas_call(
        _proj_in_kernel,
        grid=(n // tm,),
        in_specs=[row(D_MODEL)] + [full(a) for a in consts],
        out_specs=[row(w) for w, _ in out_w],
        out_shape=[jax.ShapeDtypeStruct((n, w), dt) for w, dt in out_w],
        compiler_params=_params(("parallel",), VMEM_LIMIT),
        name="proj_in",
    )(x, *consts)


def _mix_out_kernel(od_ref, og_ref, gr_ref, x_ref, gn_ref, wod_ref, wog_ref, cn_ref, wq_ref,
                    qn_ref, h_ref, qc_ref):
    og = og_ref[...]
    gr = gr_ref[...]
    gn = gn_ref[...]
    parts = []
    for h in range(GLA_HEADS):
        sl = slice(h * GLA_DV, (h + 1) * GLA_DV)
        parts.append(_rms(og[:, sl], gn[:, sl]))
    ogn = jnp.concatenate(parts, axis=-1) * (gr * jax.nn.sigmoid(gr))
    mix = (jnp.dot(od_ref[...], wod_ref[...], preferred_element_type=F32)
           + jnp.dot(ogn.astype(BF16), wog_ref[...], preferred_element_type=F32))
    h1 = x_ref[...] + mix
    h_ref[...] = h1
    hn = _rms(h1, cn_ref[...]).astype(BF16)
    q = jnp.dot(hn, wq_ref[...], preferred_element_type=F32)
    qn = qn_ref[...]
    parts = []
    for h in range(CROSS_HEADS):
        sl = slice(h * CROSS_HD, (h + 1) * CROSS_HD)
        parts.append(_rms(q[:, sl], qn[:, sl]))
    qc_ref[...] = jnp.concatenate(parts, axis=-1).astype(BF16)


def _mix_out(od, og, gr, x, p, tm=256):
    n = x.shape[0]
    tm = min(tm, n)
    row = lambda w: pl.BlockSpec((tm, w), lambda i: (i, 0))
    full = lambda a: pl.BlockSpec(a.shape, lambda i: (0,) * a.ndim)
    consts = (p["gla_out_norm"], p["w_o_d"], p["w_o_g"], p["cross_norm"], p["cross_w_q"],
              p["cross_q_norm"])
    return pl.pallas_call(
        _mix_out_kernel,
        grid=(n // tm,),
        in_specs=[row(512), row(512), row(512), row(D_MODEL)] + [full(a) for a in consts],
        out_specs=[row(D_MODEL), row(D_MODEL)],
        out_shape=[jax.ShapeDtypeStruct((n, D_MODEL), F32), jax.ShapeDtypeStruct((n, D_MODEL), BF16)],
        compiler_params=_params(("parallel",), VMEM_LIMIT),
        name="mix_out",
    )(od, og, gr, x, *consts)


def _mem_kv_kernel(m_ref, mn_ref, wkv_ref, kn_ref, k_ref, v_ref):
    mb = _rms(m_ref[...], mn_ref[...]).astype(BF16)
    kv = jnp.dot(mb, wkv_ref[...], preferred_element_type=F32)
    kn = kn_ref[...]
    parts = []
    for h in range(CROSS_HEADS):
        sl = slice(h * CROSS_HD, (h + 1) * CROSS_HD)
        parts.append(_rms(kv[:, sl], kn[:, sl]))
    k_ref[...] = jnp.concatenate(parts, axis=-1)
    v_ref[...] = kv[:, D_MODEL:]


def _mem_kv(mem, p, tm=256):
    n = mem.shape[0]
    tm = min(tm, n)
    row = lambda w: pl.BlockSpec((tm, w), lambda i: (i, 0))
    full = lambda a: pl.BlockSpec(a.shape, lambda i: (0,) * a.ndim)
    consts = (p["mem_norm"], p["cross_w_kv"], p["cross_k_norm"])
    return pl.pallas_call(
        _mem_kv_kernel,
        grid=(n // tm,),
        in_specs=[row(D_MODEL)] + [full(a) for a in consts],
        out_specs=[row(D_MODEL), row(D_MODEL)],
        out_shape=[jax.ShapeDtypeStruct((n, D_MODEL), F32)] * 2,
        compiler_params=_params(("parallel",), VMEM_LIMIT),
        name="mem_kv",
    )(mem, *consts)


def _cross_kernel(qc_ref, mk_ref, mv_ref, h1_ref, wo_ref, fn_ref, wpq_ref, pqn_ref,
                  h2_ref, xt_ref, pq_ref):
    qc = qc_ref[0]
    mk = mk_ref[0].astype(BF16)
    mv = mv_ref[0].astype(BF16)
    parts = []
    for h in range(CROSS_HEADS):
        sl = slice(h * CROSS_HD, (h + 1) * CROSS_HD)
        s = lax.dot_general(qc[:, sl], mk[:, sl], (((1,), (1,)), ((), ())),
                            preferred_element_type=F32)
        e = jnp.exp(s - jnp.max(s, axis=-1, keepdims=True))
        prob = e / jnp.sum(e, axis=-1, keepdims=True)
        parts.append(jnp.dot(prob.astype(BF16), mv[:, sl], preferred_element_type=F32))
    o = jnp.concatenate(parts, axis=-1).astype(BF16)
    h2 = h1_ref[0] + jnp.dot(o, wo_ref[...], preferred_element_type=F32)
    h2_ref[0] = h2
    xt = _rms(h2, fn_ref[...])
    xt_ref[0] = xt
    pq = jnp.dot(xt.astype(BF16), wpq_ref[...], preferred_element_type=F32)
    pqn = pqn_ref[...]
    parts = []
    for h in range(PEER_HEADS):
        sl = slice(h * PEER_QD, (h + 1) * PEER_QD)
        parts.append(_rms(pq[:, sl], pqn[:, sl]))
    pq_ref[0] = jnp.concatenate(parts, axis=-1).astype(BF16)


def _cross(qc, mk, mv, h1, p, tq=256):
    b, t, _ = qc.shape
    tq = min(tq, t)
    row = lambda w: pl.BlockSpec((1, tq, w), lambda i, j: (i, j, 0))
    mem = pl.BlockSpec((1, N_MEM, D_MODEL), lambda i, j: (i, 0, 0))
    full = lambda a: pl.BlockSpec(a.shape, lambda i, j: (0,) * a.ndim)
    consts = (p["cross_w_o"], p["ffn_norm"], p["peer_w_q"], p["peer_q_norm"])
    return pl.pallas_call(
        _cross_kernel,
        grid=(b, t // tq),
        in_specs=[row(D_MODEL), mem, mem, row(D_MODEL)] + [full(a) for a in consts],
        out_specs=[row(D_MODEL), row(D_MODEL), row(2048)],
        out_shape=[jax.ShapeDtypeStruct((b, t, D_MODEL), F32),
                   jax.ShapeDtypeStruct((b, t, D_MODEL), F32),
                   jax.ShapeDtypeStruct((b, t, 2048), BF16)],
        compiler_params=_params(("parallel", "parallel"), VMEM_LIMIT),
        name="cross_attn",
    )(qc, mk, mv, h1, *consts)


def _diff_prompt_kernel(lam_ref, qt_ref, k_ref, vt_ref, on_ref, o_ref, m_scr, l_scr, acc_scr, *, tq):
    qi = pl.program_id(1)
    lam = lam_ref[0, 0]
    cols = 4 * tq
    chan = lax.broadcasted_iota(I32, (LANES, cols), 0)
    my_map = chan // DIFF_QK == lax.broadcasted_iota(I32, (LANES, cols), 1) // tq
    causal = (lax.broadcasted_iota(I32, (tq, cols), 0)
              <= lax.broadcasted_iota(I32, (tq, cols), 1) % tq)
    top_head = lax.broadcasted_iota(I32, (LANES, tq), 0) < DIFF_V
    for p in range(DIFF_HEADS // 2):
        sl = slice(p * LANES, (p + 1) * LANES)
        qtp = qt_ref[0, sl, :]
        qs = jnp.where(my_map, jnp.concatenate([qtp] * 4, axis=1), jnp.zeros((LANES, cols), BF16))
        m_scr[...] = jnp.full((1, cols), -jnp.inf, F32)
        l_scr[...] = jnp.zeros((1, cols), F32)
        acc_scr[...] = jnp.zeros((LANES, cols), F32)

        def block(kb, masked, qs=qs):
            t0 = pl.multiple_of(kb * tq, tq)
            s = jnp.dot(k_ref[0, pl.ds(t0, tq), sl], qs, preferred_element_type=F32)
            if masked:
                s = jnp.where(causal, s, -jnp.inf)
            m_old = m_scr[...]
            m_new = jnp.maximum(m_old, jnp.max(s, axis=0, keepdims=True))
            alpha = jnp.exp(m_old - m_new)
            e = jnp.exp(s - m_new)
            m_scr[...] = m_new
            l_scr[...] = alpha * l_scr[...] + jnp.sum(e, axis=0, keepdims=True)
            acc_scr[...] = alpha * acc_scr[...] + jnp.dot(
                vt_ref[0, sl, pl.ds(t0, tq)], e.astype(BF16), preferred_element_type=F32)

        def body(kb, carry):
            block(kb, False)
            return carry

        lax.fori_loop(0, qi, body, 0)
        block(qi, True)
        on = acc_scr[...] / l_scr[...]
        ot = jnp.where(top_head, on[:, :tq] - lam * on[:, tq:2 * tq],
                       on[:, 2 * tq:3 * tq] - lam * on[:, 3 * tq:])
        sq = ot * ot
        ms = jnp.where(top_head, jnp.sum(sq[:DIFF_V], axis=0, keepdims=True),
                       jnp.sum(sq[DIFF_V:], axis=0, keepdims=True)) * (1.0 / DIFF_V)
        ot = ot * lax.rsqrt(ms + EPS) * on_ref[sl, :]
        o_ref[0, :, sl] = ot.T.astype(BF16)


def _diff_prompt(lam, dq, dk16, dv16, p, tq=256):
    b, s, _ = dq.shape
    tq = min(tq, s)
    qt = jnp.swapaxes(dq, 1, 2)
    vt = jnp.swapaxes(dv16, 1, 2)
    return pl.pallas_call(
        functools.partial(_diff_prompt_kernel, tq=tq),
        grid=(b, s // tq),
        in_specs=[pl.BlockSpec(memory_space=pltpu.SMEM),
                  pl.BlockSpec((1, DIFF_W, tq), lambda i, j: (i, 0, j)),
                  pl.BlockSpec((1, s, DIFF_W), lambda i, j: (i, 0, 0)),
                  pl.BlockSpec((1, DIFF_W, s), lambda i, j: (i, 0, 0)),
                  pl.BlockSpec((DIFF_W, 1), lambda i, j: (0, 0))],
        out_specs=pl.BlockSpec((1, tq, DIFF_W), lambda i, j: (i, j, 0)),
        out_shape=jax.ShapeDtypeStruct((b, s, DIFF_W), BF16),
        scratch_shapes=[pltpu.VMEM((1, 4 * tq), F32), pltpu.VMEM((1, 4 * tq), F32),
                        pltpu.VMEM((LANES, 4 * tq), F32)],
        compiler_params=_params(("parallel", "parallel"), VMEM_LIMIT),
        name="diff_prompt",
    )(lam, qt, dk16, vt, p["diff_out_norm"].reshape(DIFF_W, 1))


def _gla_kernel(q_ref, k_ref, v_ref, la_ref, st0_ref, tri_ref, ones_ref, o_ref, stout_ref,
                st_scr, b_scr, *, ts):
    si = pl.program_id(1)

    @pl.when(si == 0)
    def _():
        st_scr[...] = st0_ref[0]

    la = la_ref[0]
    hi = la.astype(BF16)
    r1 = la - hi.astype(F32)
    mid = r1.astype(BF16)
    lo = (r1 - mid.astype(F32)).astype(BF16)
    tri = tri_ref[...]
    b = (jnp.dot(tri, hi, preferred_element_type=F32) + jnp.dot(tri, mid, preferred_element_type=F32)
         + jnp.dot(tri, lo, preferred_element_type=F32))
    b_scr[...] = b

    q = q_ref[0]
    k = k_ref[0]
    v = v_ref[0]
    sub_q = lax.broadcasted_iota(I32, (ts, GLA_QW), 0) % GLA_SUB
    sub_v = lax.broadcasted_iota(I32, (ts, GLA_VW), 0) % GLA_SUB
    ones = ones_ref[...]
    o_intra = jnp.zeros((ts, GLA_VW), F32)
    for d in range(GLA_SUB):
        if d == 0:
            kd, bd, vd = k, b, v
        else:
            kd = pltpu.roll(k, d, 0)
            bd = pltpu.roll(b, d, 0)
            vd = pltpu.roll(v, d, 0)
        ok = sub_q >= d
        e = jnp.where(ok, q * kd * jnp.exp(jnp.where(ok, b - bd, 0.0)), 0.0)
        w = jnp.dot(e.astype(BF16), ones, preferred_element_type=F32)
        o_intra = o_intra + w * jnp.where(sub_v >= d, vd, 0.0)
    o_ref[0] = o_intra

    lane = lax.broadcasted_iota(I32, (GLA_SUB, LANES), 1)
    lane_sq = lax.broadcasted_iota(I32, (LANES, LANES), 1)

    def step(c, carry):
        t0 = pl.multiple_of(c * GLA_SUB, GLA_SUB)
        bc = b_scr[pl.ds(t0, GLA_SUB), :]
        qc = q_ref[0, pl.ds(t0, GLA_SUB), :]
        kc = k_ref[0, pl.ds(t0, GLA_SUB), :]
        vc = v_ref[0, pl.ds(t0, GLA_SUB), :]
        bl = bc[GLA_SUB - 1:GLA_SUB, :]
        qe = qc * jnp.exp(bc)
        ke = (kc * jnp.exp(bl - bc)).astype(BF16)
        dec = jnp.exp(bl)
        o_parts = []
        for p in range(GLA_HEADS // 2):
            sl = slice(p * LANES, (p + 1) * LANES)
            st = st_scr[p]
            stb = st.astype(BF16)
            qp = qe[:, sl]
            upd = []
            for hh in range(2):
                mine = (lane < GLA_DK) if hh == 0 else (lane >= GLA_DK)
                qm = jnp.where(mine, qp, 0.0).astype(BF16)
                o_parts.append(lax.dot_general(qm, stb, (((1,), (1,)), ((), ())),
                                               preferred_element_type=F32))
                vh = vc[:, (2 * p + hh) * GLA_DV:(2 * p + hh + 1) * GLA_DV]
                upd.append(jnp.dot(vh.T.astype(BF16), ke[:, sl], preferred_element_type=F32))
            st_scr[p] = dec[:, sl] * st + jnp.where(lane_sq < GLA_DK, upd[0], upd[1])
        o_ref[0, pl.ds(t0, GLA_SUB), :] += jnp.concatenate(o_parts, axis=-1)
        return carry

    lax.fori_loop(0, ts // GLA_SUB, step, 0)

    @pl.when(si == pl.num_programs(1) - 1)
    def _():
        stout_ref[0] = st_scr[...]


def _gla(gq, gk, gv, la, st0, p, ts=256):
    b, t, _ = gq.shape
    ts = min(ts, t)
    i = jnp.arange(ts)
    tri = ((i[:, None] // GLA_SUB == i[None, :] // GLA_SUB) & (i[None, :] <= i[:, None])).astype(BF16)
    ones = _block_diag_rect(GLA_QW, GLA_DK, GLA_VW, GLA_DV)
    seq = lambda w: pl.BlockSpec((1, ts, w), lambda bi, si: (bi, si, 0))
    stspec = pl.BlockSpec((1, GLA_HEADS // 2, LANES, LANES), lambda bi, si: (bi, 0, 0, 0))
    return pl.pallas_call(
        functools.partial(_gla_kernel, ts=ts),
        grid=(b, t // ts),
        in_specs=[seq(GLA_QW), seq(GLA_QW), seq(GLA_VW), seq(GLA_QW), stspec,
                  pl.BlockSpec((ts, ts), lambda bi, si: (0, 0)),
                  pl.BlockSpec((GLA_QW, GLA_VW), lambda bi, si: (0, 0))],
        out_specs=[seq(GLA_VW), stspec],
        out_shape=[jax.ShapeDtypeStruct((b, t, GLA_VW), F32),
                   jax.ShapeDtypeStruct((b, GLA_HEADS // 2, LANES, LANES), F32)],
        scratch_shapes=[pltpu.VMEM((GLA_HEADS // 2, LANES, LANES), F32),
                        pltpu.VMEM((ts, GLA_QW), F32)],
        compiler_params=_params(("parallel", "arbitrary"), VMEM_LIMIT),
        name="gla",
    )(gq, gk, gv, la, st0, tri, ones)


PAGES_PER_STEP = 8
NEW_T = 8
Q_ROWS = 2 * DIFF_HEADS * NEW_T


def _diff_sample_kernel(pt_ref, lam_ref, q_ref, *rest):
    del pt_ref
    np_ = PAGES_PER_STEP
    k_refs = rest[:np_]
    v_refs = rest[np_:2 * np_]
    kn_ref, vn_ref, on_ref, o_ref, m_scr, l_scr, acc_scr, s_scr = rest[2 * np_:]
    i = pl.program_id(1)
    nt = (((1,), (1,)), ((), ()))

    @pl.when(i == 0)
    def _():
        m_scr[...] = jnp.full(m_scr.shape, -jnp.inf, F32)
        l_scr[...] = jnp.zeros(l_scr.shape, F32)
        acc_scr[...] = jnp.zeros(acc_scr.shape, F32)

    q = q_ref[0]

    def update(state, k, v, valid):
        m, l, acc = state
        s = lax.dot_general(q, k, nt, preferred_element_type=F32)
        s = jnp.where(valid, s, -jnp.inf)
        m_new = jnp.maximum(m, jnp.max(s, axis=-1, keepdims=True))
        alpha = jnp.exp(m - m_new)
        e = jnp.exp(s - m_new)
        l = alpha * l + jnp.sum(e, axis=-1, keepdims=True)
        acc = alpha * acc + jnp.dot(e.astype(BF16), v, preferred_element_type=F32)
        return m_new, l, acc

    cols = PAGE_SIZE * DIFF_HEADS
    same_head = (lax.broadcasted_iota(I32, (Q_ROWS, cols), 1) % DIFF_HEADS
                 == lax.broadcasted_iota(I32, (Q_ROWS, cols), 0) // (2 * NEW_T))
    m_old = m_scr[...]
    m_new = m_old
    for n, k_ref in enumerate(k_refs):
        k = k_ref[...].reshape(cols, DIFF_V).astype(BF16)
        s = lax.dot_general(q, k, nt, preferred_element_type=F32)
        s = jnp.where(same_head, s, -jnp.inf)
        s_scr[n] = s
        m_new = jnp.maximum(m_new, jnp.max(s, axis=-1, keepdims=True))
    alpha = jnp.exp(m_old - m_new)
    l = alpha * l_scr[...]
    acc = alpha * acc_scr[...]
    for n, v_ref in enumerate(v_refs):
        v = v_ref[...].reshape(cols, DIFF_V).astype(BF16)
        e = jnp.exp(s_scr[n] - m_new)
        l = l + jnp.sum(e, axis=-1, keepdims=True)
        acc = acc + jnp.dot(e.astype(BF16), v, preferred_element_type=F32)
    state = (m_new, l, acc)
    m_scr[...], l_scr[...], acc_scr[...] = state

    @pl.when(i == pl.num_programs(1) - 1)
    def _():
        ncol = NEW_T * DIFF_HEADS
        row = lax.broadcasted_iota(I32, (Q_ROWS, ncol), 0)
        col = lax.broadcasted_iota(I32, (Q_ROWS, ncol), 1)
        valid = (col % DIFF_HEADS == row // (2 * NEW_T)) & (col // DIFF_HEADS <= row % NEW_T)
        _, l, acc = update(state, kn_ref[0], vn_ref[0], valid)
        on = (acc / l).reshape(DIFF_HEADS, 2 * NEW_T, DIFF_V)
        o = on[:, :NEW_T, :] - lam_ref[0, 0] * on[:, NEW_T:, :]
        ms = jnp.mean(o * o, axis=-1, keepdims=True)
        o_ref[0] = o * lax.rsqrt(ms + EPS) * on_ref[...]


def _diff_sample(lam, page_table, dq, dk16, dv16, cache_k, cache_v, p):
    db, t, _ = dq.shape
    n_pages = page_table.shape[1]
    np_ = PAGES_PER_STEP
    a = dq.reshape(db, t, DIFF_HEADS, 2, DIFF_QK).transpose(0, 2, 3, 1, 4)
    q_bd = (a[:, :, :, :, None, :] * jnp.eye(2, dtype=BF16)[None, None, :, None, :, None]
            ).reshape(db, Q_ROWS, DIFF_V)
    flat_new = lambda x: x.reshape(db, t * DIFF_HEADS, DIFF_V)

    def page_spec(n):
        return pl.BlockSpec((None, PAGE_SIZE, DIFF_HEADS, DIFF_V),
                            lambda b, i, pt: (pt[b, i * np_ + n], 0, 0, 0))

    per_b = lambda r: pl.BlockSpec((1, r, DIFF_V), lambda b, i, pt: (b, 0, 0))
    grid_spec = pltpu.PrefetchScalarGridSpec(
        num_scalar_prefetch=1,
        grid=(db, n_pages // np_),
        in_specs=[pl.BlockSpec(memory_space=pltpu.SMEM), per_b(Q_ROWS)]
        + [page_spec(n) for n in range(np_)] + [page_spec(n) for n in range(np_)]
        + [per_b(t * DIFF_HEADS), per_b(t * DIFF_HEADS),
           pl.BlockSpec((1, 1, DIFF_V), lambda b, i, pt: (0, 0, 0))],
        out_specs=pl.BlockSpec((1, DIFF_HEADS, t, DIFF_V), lambda b, i, pt: (b, 0, 0, 0)),
        scratch_shapes=[pltpu.VMEM((Q_ROWS, 1), F32), pltpu.VMEM((Q_ROWS, 1), F32),
                        pltpu.VMEM((Q_ROWS, DIFF_V), F32),
                        pltpu.VMEM((np_, Q_ROWS, PAGE_SIZE * DIFF_HEADS), F32)],
    )
    o = pl.pallas_call(
        _diff_sample_kernel,
        grid_spec=grid_spec,
        out_shape=jax.ShapeDtypeStruct((db, DIFF_HEADS, t, DIFF_V), F32),
        compiler_params=_params(("parallel", "arbitrary"), VMEM_LIMIT),
        name="diff_sample",
    )(page_table, lam, q_bd, *([cache_k] * np_), *([cache_v] * np_), flat_new(dk16),
      flat_new(dv16), p["diff_out_norm"][:, :DIFF_V].reshape(1, 1, DIFF_V))
    return o.transpose(0, 2, 1, 3).reshape(db, t, DIFF_W)


def _top16_rows(s, row):
    vals, idxs = [], []
    sentinel = s.shape[0]
    for _ in range(PEER_TOPK):
        m = jnp.max(s, axis=0, keepdims=True)
        idx = jnp.min(jnp.where(s == m, row, sentinel), axis=0, keepdims=True)
        vals.append(m)
        idxs.append(idx)
        s = jnp.where(row == idx, -jnp.inf, s)
    return jnp.concatenate(vals, axis=0), jnp.concatenate(idxs, axis=0)


def _pick_row(table, row, idx):
    return jnp.max(jnp.where(row == idx, table, -1), axis=0, keepdims=True)


_CAND_WIDTH = [PEER_TOPK // (a + 1) for a in range(PEER_TOPK)]
_CAND_ROWS = -(-sum(_CAND_WIDTH) // 8) * 8


def _route_kernel(pq_ref, sk_ref, e_ref, g_ref, *, tb):
    nt = (((1,), (1,)), ((), ()))
    row_k = lax.broadcasted_iota(I32, (N_KEYS, tb), 0)
    row_c = lax.broadcasted_iota(I32, (_CAND_ROWS, tb), 0)
    n_pad = _CAND_ROWS - sum(_CAND_WIDTH)
    e_rows, g_rows = [], []
    for h in range(PEER_HEADS):
        tops = []
        for c in range(2):
            hc = 2 * h + c
            s = lax.dot_general(sk_ref[hc], pq_ref[:, hc * N_KEYS:(hc + 1) * N_KEYS], nt,
                                preferred_element_type=F32)
            tops.append(_top16_rows(s, row_k))
        (s1, i1), (s2, i2) = tops
        cand = jnp.concatenate(
            [jnp.broadcast_to(s1[a:a + 1], (nb, tb)) + s2[:nb] for a, nb in enumerate(_CAND_WIDTH)]
            + [jnp.full((n_pad, tb), -jnp.inf, F32)], axis=0)
        cand_e = jnp.concatenate(
            [jnp.broadcast_to(i1[a:a + 1], (nb, tb)) * N_KEYS + i2[:nb]
             for a, nb in enumerate(_CAND_WIDTH)] + [jnp.zeros((n_pad, tb), I32)], axis=0)
        top_s, j = _top16_rows(cand, row_c)
        e_rows.append(jnp.concatenate(
            [_pick_row(cand_e, row_c, j[r:r + 1]) for r in range(PEER_TOPK)], axis=0))
        ex = jnp.exp(top_s - top_s[0:1])
        g_rows.append(ex / jnp.sum(ex, axis=0, keepdims=True))
    e_ref[...] = (jnp.concatenate(e_rows, axis=0) * HALF_ROWS).T
    g_ref[...] = jnp.concatenate(g_rows, axis=0).T


def _route(pq, p, tb=128):
    n = pq.shape[0]
    tb = min(tb, n)
    return pl.pallas_call(
        functools.partial(_route_kernel, tb=tb),
        grid=(n // tb,),
        in_specs=[pl.BlockSpec((tb, 2048), lambda i: (i, 0)),
                  pl.BlockSpec((2 * PEER_HEADS, N_KEYS, N_KEYS), lambda i: (0, 0, 0))],
        out_specs=[pl.BlockSpec((tb, PEER_SEL), lambda i: (i, 0))] * 2,
        out_shape=[jax.ShapeDtypeStruct((n, PEER_SEL), I32), jax.ShapeDtypeStruct((n, PEER_SEL), F32)],
        compiler_params=_params(("parallel",), VMEM_LIMIT),
        name="peer_route",
    )(pq, p["peer_sub_keys"])


HALF_ROWS = 4


def _table_row(tab_ref, row_start):
    return tab_ref[pl.ds(pl.multiple_of(row_start, HALF_ROWS), HALF_ROWS), :]


def _unpack_pair(row):
    lo = lax.bitcast_convert_type(row << 16, F32)
    hi = lax.bitcast_convert_type(row & jnp.uint32(0xFFFF0000), F32)
    return lo, hi


def _gelu_tanh(x):
    return 0.5 * x * (1.0 + jnp.tanh(math.sqrt(2.0 / math.pi) * (x + 0.044715 * (x * x * x))))


def _peer_dot_kernel(e_ref, xt_ref, g_ref, tab_ref, gsum_ref, rep_ref, w_ref, prod_scr, *, tb):
    lane = lax.broadcasted_iota(I32, (PEER_SEL, tb), 1)
    gsum = gsum_ref[...]

    def gather(t):
        xl = xt_ref[t, :HALF_ROWS, :]
        xh = xt_ref[t, HALF_ROWS:, :]
        parts = []
        for j in range(PEER_SEL):
            lo, hi = _unpack_pair(_table_row(tab_ref, e_ref[t, j]))
            parts.append(lo * xl + hi * xh)
        prod_scr[t % 4] = jnp.concatenate(parts, axis=0).astype(BF16)

    def group_sum(t):
        return jnp.dot(gsum, prod_scr[t % 4], preferred_element_type=F32)

    def lane_sum(t, per_lane, ht):
        dots = jnp.sum(per_lane, axis=1, keepdims=True)
        return jnp.where(lane == t, dots, ht)

    def token(t, carry):
        per_lane, ht = carry
        ht = lane_sum(t - 2, per_lane, ht)
        per_lane = group_sum(t - 1)
        gather(t)
        return per_lane, ht

    gather(0)
    per_lane = group_sum(0)
    gather(1)
    per_lane, ht = lax.fori_loop(2, tb, token, (per_lane, jnp.zeros((PEER_SEL, tb), F32)))
    ht = lane_sum(tb - 2, per_lane, ht)
    ht = lane_sum(tb - 1, group_sum(tb - 1), ht)
    w = g_ref[...] * _gelu_tanh(ht.T)
    w_ref[...] = jnp.dot(w.astype(BF16), rep_ref[...], preferred_element_type=F32)


def _peer_dot(e, xt3, g, tab, tb=128):
    n = e.shape[0]
    tb = min(tb, n)
    gsum = _block_diag_rect(PEER_SEL, 1, PEER_SEL * HALF_ROWS, HALF_ROWS)
    rep = _block_diag_rect(PEER_SEL, 1, PEER_SEL * 2 * HALF_ROWS, 2 * HALF_ROWS)
    return pl.pallas_call(
        functools.partial(_peer_dot_kernel, tb=tb),
        grid=(n // tb,),
        in_specs=[pl.BlockSpec((tb, PEER_SEL), lambda i: (i, 0), memory_space=pltpu.SMEM),
                  pl.BlockSpec((tb, 8, LANES), lambda i: (i, 0, 0)),
                  pl.BlockSpec((tb, PEER_SEL), lambda i: (i, 0)),
                  pl.BlockSpec(memory_space=pltpu.VMEM),
                  pl.BlockSpec(gsum.shape, lambda i: (0, 0)),
                  pl.BlockSpec(rep.shape, lambda i: (0, 0))],
        out_specs=pl.BlockSpec((tb, D_MODEL), lambda i: (i, 0)),
        out_shape=jax.ShapeDtypeStruct((n, D_MODEL), F32),
        scratch_shapes=[pltpu.VMEM((4, PEER_SEL * HALF_ROWS, LANES), BF16)],
        compiler_params=_params(("arbitrary",), VMEM_LIMIT),
        name="peer_dot",
    )(e, xt3, g, tab, gsum, rep)


def _peer_sum_kernel(e_ref, w_ref, h_ref, tab_ref, y_ref, stage0, stage1, *, tb):
    out_rows = 2 * HALF_ROWS
    q = lax.broadcasted_iota(I32, (2 * out_rows, D_MODEL), 0)
    lane = lax.broadcasted_iota(I32, (2 * out_rows, D_MODEL), 1)
    mine = (lane % out_rows == 2 * (q % HALF_ROWS) + q // HALF_ROWS) & (q < out_rows)

    def gather(t, stage):
        for j in range(PEER_SEL):
            stage[j * HALF_ROWS:(j + 1) * HALF_ROWS, :] = _table_row(tab_ref, e_ref[t, j])

    def weighted_sum(t, stage):
        wrow = w_ref[pl.ds(t, 1), :]
        lhs = jnp.where(mine, wrow, 0.0).astype(BF16)
        rows = pltpu.bitcast(stage[...], BF16)
        out = jnp.dot(lhs, rows, preferred_element_type=F32)
        y_ref[t] = h_ref[t] + out[:out_rows]

    stage0[...] = jnp.zeros(stage0.shape, stage0.dtype)
    stage1[...] = jnp.zeros(stage1.shape, stage1.dtype)

    def token_pair(i, carry):
        t0 = 2 * i
        weighted_sum(jnp.maximum(t0 - 2, 0), stage0)
        weighted_sum(jnp.maximum(t0 - 1, 0), stage1)
        gather(t0, stage0)
        gather(t0 + 1, stage1)
        return carry

    lax.fori_loop(0, tb // 2, token_pair, 0)
    weighted_sum(tb - 2, stage0)
    weighted_sum(tb - 1, stage1)


def _peer_sum(e, w, h3, tab, tb=128):
    n = e.shape[0]
    tb = min(tb, n)
    tok = lambda: pl.BlockSpec((tb, 8, LANES), lambda i: (i, 0, 0))
    stage = pltpu.VMEM((PEER_SEL * HALF_ROWS, LANES), jnp.uint32)
    return pl.pallas_call(
        functools.partial(_peer_sum_kernel, tb=tb),
        grid=(n // tb,),
        in_specs=[pl.BlockSpec((tb, PEER_SEL), lambda i: (i, 0), memory_space=pltpu.SMEM),
                  pl.BlockSpec((tb, D_MODEL), lambda i: (i, 0)), tok(),
                  pl.BlockSpec(memory_space=pltpu.VMEM)],
        out_specs=tok(),
        out_shape=jax.ShapeDtypeStruct((n, 8, LANES), F32),
        scratch_shapes=[stage, stage],
        compiler_params=_params(("arbitrary",), VMEM_LIMIT),
        name="peer_sum",
    )(e, w, h3, tab)


def _pack_table(w):
    bits = lax.bitcast_convert_type(w.astype(BF16), jnp.uint16).astype(jnp.uint32)
    half = D_MODEL // 2
    return (bits[:, :half] | (bits[:, half:] << 16)).reshape(w.shape[0] * HALF_ROWS, LANES)


def _block_diag_rect(rows, rgroup, cols, cgroup):
    r = jnp.arange(rows) // rgroup
    c = jnp.arange(cols) // cgroup
    return (r[:, None] == c[None, :]).astype(BF16)


def _state_to_pairs(s):
    b = s.shape[0]
    return s.reshape(b, 2, 2, GLA_DK, GLA_DV).transpose(0, 1, 4, 2, 3).reshape(b, 2, LANES, LANES)


def _pairs_to_state(st):
    b = st.shape[0]
    return st.reshape(b, 2, GLA_DV, 2, GLA_DK).transpose(0, 1, 3, 4, 2).reshape(
        b, GLA_HEADS, GLA_DK, GLA_DV)


def _lambda_init(layer):
    return 0.8 - 0.6 * math.exp(-0.3 * layer)


def _prep_layer(l, w):
    row = lambda a: a.reshape(1, -1).astype(F32)
    w_in = w["w_in"][l]
    lam_init = _lambda_init(l)
    return {
        "attn_norm": row(w["attn_norm"][l]),
        "w_a": w_in[:, :_OFF_GA].astype(BF16),
        "w_ga": jnp.pad(w_in[:, _OFF_GA:_OFF_GR], ((0, 0), (0, LANES - GLA_GATE_RANK))).astype(BF16),
        "w_gr": w_in[:, _OFF_GR:].astype(BF16),
        "dq_norm": row(jnp.tile(w["diff_q_norm"][l], 2 * DIFF_HEADS)) * (DIFF_QK ** -0.5),
        "dk_norm": row(jnp.tile(w["diff_k_norm"][l], 2 * DIFF_HEADS)),
        "bd32": _block_diag(DIFF_W, DIFF_QK, 1.0 / DIFF_QK),
        "w_gate": jnp.pad(w["gla_w_gate"][l], ((0, LANES - GLA_GATE_RANK), (0, 0))).astype(BF16),
        "b_gate": row(w["gla_b_gate"][l]),
        "diff_out_norm": row(jnp.tile(w["diff_out_norm"][l], DIFF_HEADS)) * (1.0 - lam_init),
        "gla_out_norm": row(jnp.tile(w["gla_out_norm"][l], GLA_HEADS)),
        "w_o_d": w["w_o"][l][:DIFF_W].astype(BF16),
        "w_o_g": w["w_o"][l][DIFF_W:].astype(BF16),
        "cross_norm": row(w["cross_norm"][l]),
        "cross_w_q": w["cross_w_q"][l].astype(BF16),
        "cross_q_norm": row(jnp.tile(w["cross_q_norm"][l], CROSS_HEADS)) * (CROSS_HD ** -0.5),
        "mem_norm": row(w["mem_norm"][l]),
        "cross_w_kv": w["cross_w_kv"][l].astype(BF16),
        "cross_k_norm": row(jnp.tile(w["cross_k_norm"][l], CROSS_HEADS)),
        "cross_w_o": w["cross_w_o"][l].astype(BF16),
        "ffn_norm": row(w["ffn_norm"][l]),
        "peer_w_q": w["peer_w_q"][l].astype(BF16),
        "peer_q_norm": row(jnp.tile(w["peer_q_norm"][l], PEER_HEADS)),
        "peer_sub_keys": w["peer_sub_keys"][l].reshape(2 * PEER_HEADS, N_KEYS, N_KEYS).astype(BF16),
        "peer_u": _pack_table(w["peer_u"][l]),
        "peer_v": _pack_table(w["peer_v"][l]),
    }


def _diff_lambda(w, l):
    f = lambda a: a[l].astype(F32)
    lam = (jnp.exp(jnp.sum(f(w["lambda_q1"]) * f(w["lambda_k1"])))
           - jnp.exp(jnp.sum(f(w["lambda_q2"]) * f(w["lambda_k2"]))) + _lambda_init(l))
    return lam.reshape(1, 1)


def _peer(h2, xt, pq, p):
    n = h2.shape[0]
    e, g = _route(pq, p)
    w = _peer_dot(e, xt.reshape(n, 8, LANES), g, p["peer_u"])
    return _peer_sum(e, w, h2.reshape(n, 8, LANES), p["peer_v"]).reshape(n, D_MODEL)


def _tail(od, og, gr, x, mk, mv, p):
    b, t, _ = x.shape
    n = b * t
    flat = lambda a: a.reshape(n, a.shape[-1])
    h1, qc = _mix_out(flat(od), flat(og), flat(gr), flat(x), p)
    h2, xt, pq = _cross(qc.reshape(b, t, D_MODEL), mk, mv, h1.reshape(b, t, D_MODEL), p)
    return _peer(flat(h2), flat(xt), flat(pq), p).reshape(b, t, D_MODEL)


def kernel(x_prompt, x_sample, mem_prompt, cache_diff_k, cache_diff_v, page_table, state_gla,
           cache_mem_k, cache_mem_v, attn_norm, w_in, diff_q_norm, diff_k_norm, lambda_q1,
           lambda_k1, lambda_q2, lambda_k2, diff_out_norm, gla_w_gate, gla_b_gate, gla_out_norm,
           w_o, cross_norm, mem_norm, cross_w_q, cross_w_kv, cross_q_norm, cross_k_norm,
           cross_w_o, ffn_norm, peer_w_q, peer_q_norm, peer_sub_keys, peer_u, peer_v):
    w = dict(attn_norm=attn_norm, w_in=w_in, diff_q_norm=diff_q_norm, diff_k_norm=diff_k_norm,
             lambda_q1=lambda_q1, lambda_k1=lambda_k1, lambda_q2=lambda_q2, lambda_k2=lambda_k2,
             diff_out_norm=diff_out_norm, gla_w_gate=gla_w_gate, gla_b_gate=gla_b_gate,
             gla_out_norm=gla_out_norm, w_o=w_o, cross_norm=cross_norm, mem_norm=mem_norm,
             cross_w_q=cross_w_q, cross_w_kv=cross_w_kv, cross_q_norm=cross_q_norm,
             cross_k_norm=cross_k_norm, cross_w_o=cross_w_o, ffn_norm=ffn_norm,
             peer_w_q=peer_w_q, peer_q_norm=peer_q_norm, peer_sub_keys=peer_sub_keys,
             peer_u=peer_u, peer_v=peer_v)
    b, s, _ = x_prompt.shape
    db, t, _ = x_sample.shape
    depth = w_in.shape[0]
    hp, hs = x_prompt, x_sample
    outs = [[] for _ in range(8)]
    for l in range(depth):
        p = _prep_layer(l, w)
        lam = _diff_lambda(w, l)

        dq, dk, dv, dk16, dv16, gq, gk, gv, la, gr = _proj_in(hp.reshape(b * s, D_MODEL), p)
        seq = lambda a: a.reshape(b, s, a.shape[-1])
        od = _diff_prompt(lam, seq(dq), seq(dk16), seq(dv16), p)
        st0 = jnp.zeros((b, GLA_HEADS // 2, LANES, LANES), F32)
        og, st = _gla(seq(gq), seq(gk), seq(gv), seq(la), st0, p)
        mk, mv = _mem_kv(mem_prompt.reshape(b * N_MEM, D_MODEL), p)
        mk = mk.reshape(b, N_MEM, D_MODEL)
        mv = mv.reshape(b, N_MEM, D_MODEL)
        hp = _tail(od, og, seq(gr), hp, mk, mv, p)
        outs[0].append(dk.reshape(b, s, DIFF_HEADS, DIFF_V))
        outs[1].append(dv.reshape(b, s, DIFF_HEADS, DIFF_V))
        outs[2].append(_pairs_to_state(st))
        outs[3].append(mk.reshape(b, N_MEM, CROSS_HEADS, CROSS_HD))
        outs[4].append(mv.reshape(b, N_MEM, CROSS_HEADS, CROSS_HD))

        dq, dk, dv, dk16, dv16, gq, gk, gv, la, gr = _proj_in(hs.reshape(db * t, D_MODEL), p)
        seq = lambda a: a.reshape(db, t, a.shape[-1])
        od = _diff_sample(lam, page_table, seq(dq), seq(dk16), seq(dv16), cache_diff_k[l],
                          cache_diff_v[l], p)
        tp = -(-t // GLA_SUB) * GLA_SUB
        padt = lambda a: jnp.pad(seq(a), ((0, 0), (0, tp - t), (0, 0)))
        og, st = _gla(padt(gq), padt(gk), padt(gv), padt(la), _state_to_pairs(state_gla[l]), p)
        hs_pad = jnp.pad(hs, ((0, 0), (0, tp - t), (0, 0)))
        y = _tail(jnp.pad(od, ((0, 0), (0, tp - t), (0, 0))).astype(BF16), og, padt(gr), hs_pad,
                  cache_mem_k[l].reshape(db, N_MEM, D_MODEL),
                  cache_mem_v[l].reshape(db, N_MEM, D_MODEL), p)
        hs = y[:, :t]
        outs[5].append(dk.reshape(db, t, DIFF_HEADS, DIFF_V))
        outs[6].append(dv.reshape(db, t, DIFF_HEADS, DIFF_V))
        outs[7].append(_pairs_to_state(st))

    stack = lambda xs: jnp.stack(xs)
    return (hp, hs, stack(outs[0]), stack(outs[1]), stack(outs[2]), stack(outs[3]), stack(outs[4]),
            stack(outs[5]), stack(outs[6]), stack(outs[7]))
```

```python
import functools
import math

import jax
import jax.numpy as jnp
from jax import lax
from jax.experimental import pallas as pl
from jax.experimental.pallas import tpu as pltpu

F32 = jnp.float32
BF16 = jnp.bfloat16
I32 = jnp.int32

EPS = 1e-6
D_MODEL = 1024
DIFF_HEADS = 8
DIFF_QK = 32
DIFF_V = 64
DIFF_W = DIFF_HEADS * DIFF_V
GLA_HEADS = 4
GLA_DK = 64
GLA_DV = 128
GLA_QW = GLA_HEADS * GLA_DK
GLA_VW = GLA_HEADS * GLA_DV
GLA_GATE_RANK = 16
GLA_TAU = 16.0
GLA_SUB = 16
CROSS_HEADS = 4
CROSS_HD = 256
N_MEM = 256
PEER_HEADS = 8
PEER_QD = 256
N_KEYS = 128
PEER_TOPK = 16
PEER_SEL = PEER_HEADS * PEER_TOPK
PAGE_SIZE = 128
LANES = 128
VMEM_LIMIT = 56 * 1024 * 1024

_OFF_GA = 2560
_OFF_GR = 2576


def _params(sem, vmem=None):
    return pltpu.CompilerParams(dimension_semantics=sem, vmem_limit_bytes=vmem)


def _rms(x, g):
    return x * lax.rsqrt(jnp.mean(x * x, axis=-1, keepdims=True) + EPS) * g


def _split_dot(x, w):
    hi = x.astype(BF16)
    lo = (x - hi.astype(F32)).astype(BF16)
    return (jnp.dot(hi, w, preferred_element_type=F32)
            + jnp.dot(lo, w, preferred_element_type=F32))


def _group_rms(x, bd, g):
    ms = _split_dot(x * x, bd)
    return x * lax.rsqrt(ms + EPS) * g


def _block_diag(width, group, value, dtype=BF16):
    i = jnp.arange(width) // group
    return jnp.where(i[:, None] == i[None, :], value, 0.0).astype(dtype)


def _proj_in_kernel(x_ref, g_ref, wa_ref, wgr_ref, wga_ref, qn_ref, kn_ref, bd_ref, wg_ref, bg_ref,
                    dq_ref, dk_ref, dv_ref, dk16_ref, dv16_ref, gq_ref, gk_ref, gv_ref, la_ref,
                    gr_ref):
    xb = _rms(x_ref[...], g_ref[...]).astype(BF16)
    ya = jnp.dot(xb, wa_ref[...], preferred_element_type=F32)
    bd = bd_ref[...]
    dq = _group_rms(ya[:, 0:512], bd, qn_ref[...])
    dk = _group_rms(ya[:, 512:1024], bd, kn_ref[...])
    dv = ya[:, 1024:1536]
    dq_ref[...] = dq.astype(BF16)
    dk_ref[...] = dk
    dv_ref[...] = dv
    dk16_ref[...] = dk.astype(BF16)
    dv16_ref[...] = dv.astype(BF16)
    gq_ref[...] = ya[:, 1536:1792] * (GLA_DK ** -0.5)
    gk_ref[...] = ya[:, 1792:2048]
    gv_ref[...] = ya[:, 2048:2560]
    gr_ref[...] = jnp.dot(xb, wgr_ref[...], preferred_element_type=F32)
    ga = jnp.dot(xb, wga_ref[...], preferred_element_type=F32)
    z = jnp.dot(ga.astype(BF16), wg_ref[...], preferred_element_type=F32) + bg_ref[...]
    log_sig = jnp.minimum(z, 0.0) - jnp.log(1.0 + jnp.exp(-jnp.abs(z)))
    la_ref[...] = log_sig * (1.0 / GLA_TAU)


def _proj_in(x, p, tm=256):
    n = x.shape[0]
    tm = min(tm, n)
    row = lambda w: pl.BlockSpec((tm, w), lambda i: (i, 0))
    full = lambda a: pl.BlockSpec(a.shape, lambda i: (0,) * a.ndim)
    consts = (p["attn_norm"], p["w_a"], p["w_gr"], p["w_ga"], p["dq_norm"], p["dk_norm"],
              p["bd32"], p["w_gate"], p["b_gate"])
    out_w = ((512, BF16), (512, F32), (512, F32), (512, BF16), (512, BF16), (256, F32), (256, F32),
             (512, F32), (256, F32), (512, F32))
    return pl.pallas_call(
        _proj_in_kernel,
        grid=(n // tm,),
        in_specs=[row(D_MODEL)] + [full(a) for a in consts],
        out_specs=[row(w) for w, _ in out_w],
        out_shape=[jax.ShapeDtypeStruct((n, w), dt) for w, dt in out_w],
        compiler_params=_params(("parallel",), VMEM_LIMIT),
        name="proj_in",
    )(x, *consts)


def _mix_out_kernel(od_ref, og_ref, gr_ref, x_ref, gn_ref, wod_ref, wog_ref, cn_ref, wq_ref,
                    qn_ref, h_ref, qc_ref):
    og = og_ref[...]
    gr = gr_ref[...]
    gn = gn_ref[...]
    parts = []
    for h in range(GLA_HEADS):
        sl = slice(h * GLA_DV, (h + 1) * GLA_DV)
        parts.append(_rms(og[:, sl], gn[:, sl]))
    ogn = jnp.concatenate(parts, axis=-1) * (gr * jax.nn.sigmoid(gr))
    mix = (jnp.dot(od_ref[...], wod_ref[...], preferred_element_type=F32)
           + jnp.dot(ogn.astype(BF16), wog_ref[...], preferred_element_type=F32))
    h1 = x_ref[...] + mix
    h_ref[...] = h1
    hn = _rms(h1, cn_ref[...]).astype(BF16)
    q = jnp.dot(hn, wq_ref[...], preferred_element_type=F32)
    qn = qn_ref[...]
    parts = []
    for h in range(CROSS_HEADS):
        sl = slice(h * CROSS_HD, (h + 1) * CROSS_HD)
        parts.append(_rms(q[:, sl], qn[:, sl]))
    qc_ref[...] = jnp.concatenate(parts, axis=-1).astype(BF16)


def _mix_out(od, og, gr, x, p, tm=256):
    n = x.shape[0]
    tm = min(tm, n)
    row = lambda w: pl.BlockSpec((tm, w), lambda i: (i, 0))
    full = lambda a: pl.BlockSpec(a.shape, lambda i: (0,) * a.ndim)
    consts = (p["gla_out_norm"], p["w_o_d"], p["w_o_g"], p["cross_norm"], p["cross_w_q"],
              p["cross_q_norm"])
    return pl.pallas_call(
        _mix_out_kernel,
        grid=(n // tm,),
        in_specs=[row(512), row(512), row(512), row(D_MODEL)] + [full(a) for a in consts],
        out_specs=[row(D_MODEL), row(D_MODEL)],
        out_shape=[jax.ShapeDtypeStruct((n, D_MODEL), F32), jax.ShapeDtypeStruct((n, D_MODEL), BF16)],
        compiler_params=_params(("parallel",), VMEM_LIMIT),
        name="mix_out",
    )(od, og, gr, x, *consts)


def _mem_kv_kernel(m_ref, mn_ref, wkv_ref, kn_ref, k_ref, v_ref):
    mb = _rms(m_ref[...], mn_ref[...]).astype(BF16)
    kv = jnp.dot(mb, wkv_ref[...], preferred_element_type=F32)
    kn = kn_ref[...]
    parts = []
    for h in range(CROSS_HEADS):
        sl = slice(h * CROSS_HD, (h + 1) * CROSS_HD)
        parts.append(_rms(kv[:, sl], kn[:, sl]))
    k_ref[...] = jnp.concatenate(parts, axis=-1)
    v_ref[...] = kv[:, D_MODEL:]


def _mem_kv(mem, p, tm=256):
    n = mem.shape[0]
    tm = min(tm, n)
    row = lambda w: pl.BlockSpec((tm, w), lambda i: (i, 0))
    full = lambda a: pl.BlockSpec(a.shape, lambda i: (0,) * a.ndim)
    consts = (p["mem_norm"], p["cross_w_kv"], p["cross_k_norm"])
    return pl.pallas_call(
        _mem_kv_kernel,
        grid=(n // tm,),
        in_specs=[row(D_MODEL)] + [full(a) for a in consts],
        out_specs=[row(D_MODEL), row(D_MODEL)],
        out_shape=[jax.ShapeDtypeStruct((n, D_MODEL), F32)] * 2,
        compiler_params=_params(("parallel",), VMEM_LIMIT),
        name="mem_kv",
    )(mem, *consts)


def _cross_kernel(qc_ref, mk_ref, mv_ref, h1_ref, wo_ref, fn_ref, wpq_ref, pqn_ref,
                  h2_ref, xt_ref, pq_ref):
    qc = qc_ref[0]
    mk = mk_ref[0].astype(BF16)
    mv = mv_ref[0].astype(BF16)
    parts = []
    for h in range(CROSS_HEADS):
        sl = slice(h * CROSS_HD, (h + 1) * CROSS_HD)
        s = lax.dot_general(qc[:, sl], mk[:, sl], (((1,), (1,)), ((), ())),
                            preferred_element_type=F32)
        e = jnp.exp(s - jnp.max(s, axis=-1, keepdims=True))
        prob = e / jnp.sum(e, axis=-1, keepdims=True)
        parts.append(jnp.dot(prob.astype(BF16), mv[:, sl], preferred_element_type=F32))
    o = jnp.concatenate(parts, axis=-1).astype(BF16)
    h2 = h1_ref[0] + jnp.dot(o, wo_ref[...], preferred_element_type=F32)
    h2_ref[0] = h2
    xt = _rms(h2, fn_ref[...])
    xt_ref[0] = xt
    pq = jnp.dot(xt.astype(BF16), wpq_ref[...], preferred_element_type=F32)
    pqn = pqn_ref[...]
    parts = []
    for h in range(PEER_HEADS):
        sl = slice(h * PEER_QD, (h + 1) * PEER_QD)
        parts.append(_rms(pq[:, sl], pqn[:, sl]))
    pq_ref[0] = jnp.concatenate(parts, axis=-1).astype(BF16)


def _cross(qc, mk, mv, h1, p, tq=256):
    b, t, _ = qc.shape
    tq = min(tq, t)
    row = lambda w: pl.BlockSpec((1, tq, w), lambda i, j: (i, j, 0))
    mem = pl.BlockSpec((1, N_MEM, D_MODEL), lambda i, j: (i, 0, 0))
    full = lambda a: pl.BlockSpec(a.shape, lambda i, j: (0,) * a.ndim)
    consts = (p["cross_w_o"], p["ffn_norm"], p["peer_w_q"], p["peer_q_norm"])
    return pl.pallas_call(
        _cross_kernel,
        grid=(b, t // tq),
        in_specs=[row(D_MODEL), mem, mem, row(D_MODEL)] + [full(a) for a in consts],
        out_specs=[row(D_MODEL), row(D_MODEL), row(2048)],
        out_shape=[jax.ShapeDtypeStruct((b, t, D_MODEL), F32),
                   jax.ShapeDtypeStruct((b, t, D_MODEL), F32),
                   jax.ShapeDtypeStruct((b, t, 2048), BF16)],
        compiler_params=_params(("parallel", "parallel"), VMEM_LIMIT),
        name="cross_attn",
    )(qc, mk, mv, h1, *consts)


def _diff_prompt_kernel(lam_ref, qt_ref, k_ref, vt_ref, on_ref, o_ref, m_scr, l_scr, acc_scr, *, tq):
    qi = pl.program_id(1)
    lam = lam_ref[0, 0]
    cols = 4 * tq
    chan = lax.broadcasted_iota(I32, (LANES, cols), 0)
    my_map = chan // DIFF_QK == lax.broadcasted_iota(I32, (LANES, cols), 1) // tq
    causal = (lax.broadcasted_iota(I32, (tq, cols), 0)
              <= lax.broadcasted_iota(I32, (tq, cols), 1) % tq)
    top_head = lax.broadcasted_iota(I32, (LANES, tq), 0) < DIFF_V
    for p in range(DIFF_HEADS // 2):
        sl = slice(p * LANES, (p + 1) * LANES)
        qtp = qt_ref[0, sl, :]
        qs = jnp.where(my_map, jnp.concatenate([qtp] * 4, axis=1), jnp.zeros((LANES, cols), BF16))
        m_scr[...] = jnp.full((1, cols), -jnp.inf, F32)
        l_scr[...] = jnp.zeros((1, cols), F32)
        acc_scr[...] = jnp.zeros((LANES, cols), F32)

        def block(kb, masked, qs=qs):
            t0 = pl.multiple_of(kb * tq, tq)
            s = jnp.dot(k_ref[0, pl.ds(t0, tq), sl], qs, preferred_element_type=F32)
            if masked:
                s = jnp.where(causal, s, -jnp.inf)
            m_old = m_scr[...]
            m_new = jnp.maximum(m_old, jnp.max(s, axis=0, keepdims=True))
            alpha = jnp.exp(m_old - m_new)
            e = jnp.exp(s - m_new)
            m_scr[...] = m_new
            l_scr[...] = alpha * l_scr[...] + jnp.sum(e, axis=0, keepdims=True)
            acc_scr[...] = alpha * acc_scr[...] + jnp.dot(
                vt_ref[0, sl, pl.ds(t0, tq)], e.astype(BF16), preferred_element_type=F32)

        def body(kb, carry):
            block(kb, False)
            return carry

        lax.fori_loop(0, qi, body, 0)
        block(qi, True)
        on = acc_scr[...] / l_scr[...]
        ot = jnp.where(top_head, on[:, :tq] - lam * on[:, tq:2 * tq],
                       on[:, 2 * tq:3 * tq] - lam * on[:, 3 * tq:])
        sq = ot * ot
        ms = jnp.where(top_head, jnp.sum(sq[:DIFF_V], axis=0, keepdims=True),
                       jnp.sum(sq[DIFF_V:], axis=0, keepdims=True)) * (1.0 / DIFF_V)
        ot = ot * lax.rsqrt(ms + EPS) * on_ref[sl, :]
        o_ref[0, :, sl] = ot.T.astype(BF16)


def _diff_prompt(lam, dq, dk16, dv16, p, tq=256):
    b, s, _ = dq.shape
    tq = min(tq, s)
    qt = jnp.swapaxes(dq, 1, 2)
    vt = jnp.swapaxes(dv16, 1, 2)
    return pl.pallas_call(
        functools.partial(_diff_prompt_kernel, tq=tq),
        grid=(b, s // tq),
        in_specs=[pl.BlockSpec(memory_space=pltpu.SMEM),
                  pl.BlockSpec((1, DIFF_W, tq), lambda i, j: (i, 0, j)),
                  pl.BlockSpec((1, s, DIFF_W), lambda i, j: (i, 0, 0)),
                  pl.BlockSpec((1, DIFF_W, s), lambda i, j: (i, 0, 0)),
                  pl.BlockSpec((DIFF_W, 1), lambda i, j: (0, 0))],
        out_specs=pl.BlockSpec((1, tq, DIFF_W), lambda i, j: (i, j, 0)),
        out_shape=jax.ShapeDtypeStruct((b, s, DIFF_W), BF16),
        scratch_shapes=[pltpu.VMEM((1, 4 * tq), F32), pltpu.VMEM((1, 4 * tq), F32),
                        pltpu.VMEM((LANES, 4 * tq), F32)],
        compiler_params=_params(("parallel", "parallel"), VMEM_LIMIT),
        name="diff_prompt",
    )(lam, qt, dk16, vt, p["diff_out_norm"].reshape(DIFF_W, 1))


def _gla_kernel(q_ref, k_ref, v_ref, la_ref, st0_ref, tri_ref, ones_ref, o_ref, stout_ref,
                st_scr, b_scr, *, ts):
    si = pl.program_id(1)

    @pl.when(si == 0)
    def _():
        st_scr[...] = st0_ref[0]

    la = la_ref[0]
    hi = la.astype(BF16)
    r1 = la - hi.astype(F32)
    mid = r1.astype(BF16)
    lo = (r1 - mid.astype(F32)).astype(BF16)
    tri = tri_ref[...]
    b = (jnp.dot(tri, hi, preferred_element_type=F32) + jnp.dot(tri, mid, preferred_element_type=F32)
         + jnp.dot(tri, lo, preferred_element_type=F32))
    b_scr[...] = b

    q = q_ref[0]
    k = k_ref[0]
    v = v_ref[0]
    sub_q = lax.broadcasted_iota(I32, (ts, GLA_QW), 0) % GLA_SUB
    sub_v = lax.broadcasted_iota(I32, (ts, GLA_VW), 0) % GLA_SUB
    ones = ones_ref[...]
    o_intra = jnp.zeros((ts, GLA_VW), F32)
    for d in range(GLA_SUB):
        if d == 0:
            kd, bd, vd = k, b, v
        else:
            kd = pltpu.roll(k, d, 0)
            bd = pltpu.roll(b, d, 0)
            vd = pltpu.roll(v, d, 0)
        ok = sub_q >= d
        e = jnp.where(ok, q * kd * jnp.exp(jnp.where(ok, b - bd, 0.0)), 0.0)
        w = jnp.dot(e.astype(BF16), ones, preferred_element_type=F32)
        o_intra = o_intra + w * jnp.where(sub_v >= d, vd, 0.0)
    o_ref[0] = o_intra

    lane = lax.broadcasted_iota(I32, (GLA_SUB, LANES), 1)
    lane_sq = lax.broadcasted_iota(I32, (LANES, LANES), 1)

    def step(c, carry):
        t0 = pl.multiple_of(c * GLA_SUB, GLA_SUB)
        bc = b_scr[pl.ds(t0, GLA_SUB), :]
        qc = q_ref[0, pl.ds(t0, GLA_SUB), :]
        kc = k_ref[0, pl.ds(t0, GLA_SUB), :]
        vc = v_ref[0, pl.ds(t0, GLA_SUB), :]
        bl = bc[GLA_SUB - 1:GLA_SUB, :]
        qe = qc * jnp.exp(bc)
        ke = (kc * jnp.exp(bl - bc)).astype(BF16)
        dec = jnp.exp(bl)
        o_parts = []
        for p in range(GLA_HEADS // 2):
            sl = slice(p * LANES, (p + 1) * LANES)
            st = st_scr[p]
            stb = st.astype(BF16)
            qp = qe[:, sl]
            upd = []
            for hh in range(2):
                mine = (lane < GLA_DK) if hh == 0 else (lane >= GLA_DK)
                qm = jnp.where(mine, qp, 0.0).astype(BF16)
                o_parts.append(lax.dot_general(qm, stb, (((1,), (1,)), ((), ())),
                                               preferred_element_type=F32))
                vh = vc[:, (2 * p + hh) * GLA_DV:(2 * p + hh + 1) * GLA_DV]
                upd.append(jnp.dot(vh.T.astype(BF16), ke[:, sl], preferred_element_type=F32))
            st_scr[p] = dec[:, sl] * st + jnp.where(lane_sq < GLA_DK, upd[0], upd[1])
        o_ref[0, pl.ds(t0, GLA_SUB), :] += jnp.concatenate(o_parts, axis=-1)
        return carry

    lax.fori_loop(0, ts // GLA_SUB, step, 0)

    @pl.when(si == pl.num_programs(1) - 1)
    def _():
        stout_ref[0] = st_scr[...]


def _gla(gq, gk, gv, la, st0, p, ts=256):
    b, t, _ = gq.shape
    ts = min(ts, t)
    i = jnp.arange(ts)
    tri = ((i[:, None] // GLA_SUB == i[None, :] // GLA_SUB) & (i[None, :] <= i[:, None])).astype(BF16)
    ones = _block_diag_rect(GLA_QW, GLA_DK, GLA_VW, GLA_DV)
    seq = lambda w: pl.BlockSpec((1, ts, w), lambda bi, si: (bi, si, 0))
    stspec = pl.BlockSpec((1, GLA_HEADS // 2, LANES, LANES), lambda bi, si: (bi, 0, 0, 0))
    return pl.pallas_call(
        functools.partial(_gla_kernel, ts=ts),
        grid=(b, t // ts),
        in_specs=[seq(GLA_QW), seq(GLA_QW), seq(GLA_VW), seq(GLA_QW), stspec,
                  pl.BlockSpec((ts, ts), lambda bi, si: (0, 0)),
                  pl.BlockSpec((GLA_QW, GLA_VW), lambda bi, si: (0, 0))],
        out_specs=[seq(GLA_VW), stspec],
        out_shape=[jax.ShapeDtypeStruct((b, t, GLA_VW), F32),
                   jax.ShapeDtypeStruct((b, GLA_HEADS // 2, LANES, LANES), F32)],
        scratch_shapes=[pltpu.VMEM((GLA_HEADS // 2, LANES, LANES), F32),
                        pltpu.VMEM((ts, GLA_QW), F32)],
        compiler_params=_params(("parallel", "arbitrary"), VMEM_LIMIT),
        name="gla",
    )(gq, gk, gv, la, st0, tri, ones)


PAGES_PER_STEP = 8
NEW_T = 8


def _diff_sample_kernel(pt_ref, lam_ref, q_ref, *rest):
    del pt_ref
    np_ = PAGES_PER_STEP
    k_refs = rest[:np_]
    v_refs = rest[np_:2 * np_]
    kn_ref, vn_ref, on_ref, o_ref, m_scr, l_scr, acc_scr = rest[2 * np_:]
    i = pl.program_id(1)
    nt = (((1,), (1,)), ((), ()))

    @pl.when(i == 0)
    def _():
        m_scr[...] = jnp.full(m_scr.shape, -jnp.inf, F32)
        l_scr[...] = jnp.zeros(l_scr.shape, F32)
        acc_scr[...] = jnp.zeros(acc_scr.shape, F32)

    q = q_ref[0]

    def update(scores, value_products):
        m_old = m_scr[...]
        m_new = m_old
        for s in scores:
            m_new = jnp.maximum(m_new, jnp.max(s, axis=-1, keepdims=True))
        alpha = jnp.exp(m_old - m_new)
        l = alpha * l_scr[...]
        acc = alpha * acc_scr[...]
        for s, pv in zip(scores, value_products):
            e = jnp.exp(s - m_new)
            l = l + jnp.sum(e, axis=-1, keepdims=True)
            acc = acc + pv(e.astype(BF16))
        m_scr[...] = m_new
        l_scr[...] = l
        acc_scr[...] = acc

    def page_t(ref):
        return ref[...].reshape(DIFF_W, PAGE_SIZE).astype(BF16)

    scores = [jnp.dot(q, page_t(k_ref), preferred_element_type=F32) for k_ref in k_refs]
    update(scores, [lambda e, v_ref=v_ref: lax.dot_general(e, page_t(v_ref), nt,
                                                           preferred_element_type=F32)
                    for v_ref in v_refs])

    @pl.when(i == pl.num_programs(1) - 1)
    def _():
        s = lax.dot_general(q, kn_ref[0], nt, preferred_element_type=F32)
        t_q = lax.broadcasted_iota(I32, s.shape, 0) % NEW_T
        t_k = lax.broadcasted_iota(I32, s.shape, 1)
        update([jnp.where(t_k <= t_q, s, -jnp.inf)],
               [lambda e: jnp.dot(e, vn_ref[0], preferred_element_type=F32)])
        accn = acc_scr[...] / l_scr[...]
        rows = 2 * NEW_T
        lane = lax.broadcasted_iota(I32, (rows, DIFF_W), 1)
        o16 = jnp.zeros((rows, DIFF_W), F32)
        for h in range(DIFF_HEADS):
            o16 = jnp.where(lane // DIFF_V == h, accn[h * rows:(h + 1) * rows, :], o16)
        o = o16[:NEW_T] - lam_ref[0, 0] * o16[NEW_T:]
        lane = lax.broadcasted_iota(I32, (NEW_T, DIFF_W), 1)
        sq = o * o
        ms = jnp.zeros_like(o)
        for h in range(DIFF_HEADS):
            mine = lane // DIFF_V == h
            ms = jnp.where(mine, jnp.sum(jnp.where(mine, sq, 0.0), axis=-1, keepdims=True), ms)
        o_ref[0] = o * lax.rsqrt(ms * (1.0 / DIFF_V) + EPS) * on_ref[...]


def _diff_sample(lam, page_table, dq, dk16, dv16, cache_k, cache_v, p):
    db, t, _ = dq.shape
    n_pages = page_table.shape[1]
    np_ = PAGES_PER_STEP
    a = dq.reshape(db, t, 16, DIFF_QK).transpose(0, 2, 1, 3)
    q_bd = (a[:, :, :, None, :] * jnp.eye(16, dtype=BF16)[None, :, None, :, None]
            ).reshape(db, 16 * t, DIFF_W)
    pad = lambda x: jnp.pad(x, ((0, 0), (0, LANES - t), (0, 0)))
    ck = cache_k.transpose(0, 2, 3, 1)
    cv = cache_v.transpose(0, 2, 3, 1)

    def page_spec(n):
        return pl.BlockSpec((None, DIFF_HEADS, DIFF_V, PAGE_SIZE),
                            lambda b, i, pt: (pt[b, i * np_ + n], 0, 0, 0))

    per_b = lambda r: pl.BlockSpec((1, r, DIFF_W), lambda b, i, pt: (b, 0, 0))
    grid_spec = pltpu.PrefetchScalarGridSpec(
        num_scalar_prefetch=1,
        grid=(db, n_pages // np_),
        in_specs=[pl.BlockSpec(memory_space=pltpu.SMEM), per_b(16 * t)]
        + [page_spec(n) for n in range(np_)] + [page_spec(n) for n in range(np_)]
        + [per_b(LANES), per_b(LANES), pl.BlockSpec((1, DIFF_W), lambda b, i, pt: (0, 0))],
        out_specs=per_b(t),
        scratch_shapes=[pltpu.VMEM((16 * t, 1), F32), pltpu.VMEM((16 * t, 1), F32),
                        pltpu.VMEM((16 * t, DIFF_W), F32)],
    )
    return pl.pallas_call(
        _diff_sample_kernel,
        grid_spec=grid_spec,
        out_shape=jax.ShapeDtypeStruct((db, t, DIFF_W), F32),
        compiler_params=_params(("parallel", "arbitrary"), VMEM_LIMIT),
        name="diff_sample",
    )(page_table, lam, q_bd, *([ck] * np_), *([cv] * np_), pad(dk16), pad(dv16),
      p["diff_out_norm"])


def _top16_rows(s, row):
    vals, idxs = [], []
    sentinel = s.shape[0]
    for _ in range(PEER_TOPK):
        m = jnp.max(s, axis=0, keepdims=True)
        idx = jnp.min(jnp.where(s == m, row, sentinel), axis=0, keepdims=True)
        vals.append(m)
        idxs.append(idx)
        s = jnp.where(row == idx, -jnp.inf, s)
    return jnp.concatenate(vals, axis=0), jnp.concatenate(idxs, axis=0)


def _pick_row(table, row, idx):
    return jnp.max(jnp.where(row == idx, table, -1), axis=0, keepdims=True)


_CAND_WIDTH = [PEER_TOPK // (a + 1) for a in range(PEER_TOPK)]
_CAND_ROWS = -(-sum(_CAND_WIDTH) // 8) * 8


def _route_kernel(pq_ref, sk_ref, e_ref, g_ref, *, tb):
    nt = (((1,), (1,)), ((), ()))
    row_k = lax.broadcasted_iota(I32, (N_KEYS, tb), 0)
    row_c = lax.broadcasted_iota(I32, (_CAND_ROWS, tb), 0)
    n_pad = _CAND_ROWS - sum(_CAND_WIDTH)
    e_rows, g_rows = [], []
    for h in range(PEER_HEADS):
        tops = []
        for c in range(2):
            hc = 2 * h + c
            s = lax.dot_general(sk_ref[hc], pq_ref[:, hc * N_KEYS:(hc + 1) * N_KEYS], nt,
                                preferred_element_type=F32)
            tops.append(_top16_rows(s, row_k))
        (s1, i1), (s2, i2) = tops
        cand = jnp.concatenate(
            [jnp.broadcast_to(s1[a:a + 1], (nb, tb)) + s2[:nb] for a, nb in enumerate(_CAND_WIDTH)]
            + [jnp.full((n_pad, tb), -jnp.inf, F32)], axis=0)
        cand_e = jnp.concatenate(
            [jnp.broadcast_to(i1[a:a + 1], (nb, tb)) * N_KEYS + i2[:nb]
             for a, nb in enumerate(_CAND_WIDTH)] + [jnp.zeros((n_pad, tb), I32)], axis=0)
        top_s, j = _top16_rows(cand, row_c)
        e_rows.append(jnp.concatenate(
            [_pick_row(cand_e, row_c, j[r:r + 1]) for r in range(PEER_TOPK)], axis=0))
        ex = jnp.exp(top_s - top_s[0:1])
        g_rows.append(ex / jnp.sum(ex, axis=0, keepdims=True))
    e_ref[...] = (jnp.concatenate(e_rows, axis=0) * HALF_ROWS).T
    g_ref[...] = jnp.concatenate(g_rows, axis=0).T


def _route(pq, p, tb=128):
    n = pq.shape[0]
    tb = min(tb, n)
    return pl.pallas_call(
        functools.partial(_route_kernel, tb=tb),
        grid=(n // tb,),
        in_specs=[pl.BlockSpec((tb, 2048), lambda i: (i, 0)),
                  pl.BlockSpec((2 * PEER_HEADS, N_KEYS, N_KEYS), lambda i: (0, 0, 0))],
        out_specs=[pl.BlockSpec((tb, PEER_SEL), lambda i: (i, 0))] * 2,
        out_shape=[jax.ShapeDtypeStruct((n, PEER_SEL), I32), jax.ShapeDtypeStruct((n, PEER_SEL), F32)],
        compiler_params=_params(("parallel",), VMEM_LIMIT),
        name="peer_route",
    )(pq, p["peer_sub_keys"])


HALF_ROWS = 4


def _table_row(tab_ref, row_start):
    return tab_ref[pl.ds(pl.multiple_of(row_start, HALF_ROWS), HALF_ROWS), :]


def _unpack_pair(row):
    lo = lax.bitcast_convert_type(row << 16, F32)
    hi = lax.bitcast_convert_type(row & jnp.uint32(0xFFFF0000), F32)
    return lo, hi


def _gelu_tanh(x):
    return 0.5 * x * (1.0 + jnp.tanh(math.sqrt(2.0 / math.pi) * (x + 0.044715 * (x * x * x))))


def _gather_rows(e_ref, tab_ref, t, stage):
    for j in range(PEER_SEL):
        stage[j * HALF_ROWS:(j + 1) * HALF_ROWS, :] = _table_row(tab_ref, e_ref[t, j])


def _peer_dot_kernel(e_ref, xt_ref, g_ref, tab_ref, gsum_ref, w_ref, stage0, stage1,
                     part0, part1, *, tb):
    lane = lax.broadcasted_iota(I32, (PEER_SEL, tb), 1)
    gsum = gsum_ref[...]
    chunks = 2 * HALF_ROWS
    pairs = PEER_SEL * chunks // 16

    def group_sums(t, stage, part):
        x = xt_ref[jnp.maximum(t, 0)]
        rows = pltpu.bitcast(stage[...], BF16).reshape(pairs, 16, LANES)
        prod = (rows * x[None]).reshape(PEER_SEL * chunks, LANES)
        part[...] = jnp.dot(gsum, prod, preferred_element_type=F32)

    def lane_sums(t, part, ht):
        return jnp.where(lane == t, jnp.sum(part[...], axis=1, keepdims=True), ht)

    for ref in (stage0, stage1, part0, part1):
        ref[...] = jnp.zeros(ref.shape, ref.dtype)

    def token_pair(i, ht):
        t0 = 2 * i
        ht = lane_sums(t0 - 4, part0, ht)
        ht = lane_sums(t0 - 3, part1, ht)
        group_sums(t0 - 2, stage0, part0)
        group_sums(t0 - 1, stage1, part1)
        _gather_rows(e_ref, tab_ref, t0, stage0)
        _gather_rows(e_ref, tab_ref, t0 + 1, stage1)
        return ht

    ht = lax.fori_loop(0, tb // 2, token_pair, jnp.zeros((PEER_SEL, tb), F32))
    ht = lane_sums(tb - 4, part0, ht)
    ht = lane_sums(tb - 3, part1, ht)
    group_sums(tb - 2, stage0, part0)
    group_sums(tb - 1, stage1, part1)
    ht = lane_sums(tb - 2, part0, ht)
    ht = lane_sums(tb - 1, part1, ht)
    w = g_ref[...] * _gelu_tanh(ht.T)
    w_ref[...] = jnp.dot(w.astype(BF16), gsum, preferred_element_type=F32)


def _peer_dot(e, xt3, g, tab, tb=128):
    n = e.shape[0]
    tb = min(tb, n)
    gsum = _block_diag_rect(PEER_SEL, 1, PEER_SEL * 2 * HALF_ROWS, 2 * HALF_ROWS)
    return pl.pallas_call(
        functools.partial(_peer_dot_kernel, tb=tb),
        grid=(n // tb,),
        in_specs=[pl.BlockSpec((tb, PEER_SEL), lambda i: (i, 0), memory_space=pltpu.SMEM),
                  pl.BlockSpec((tb, 16, LANES), lambda i: (i, 0, 0)),
                  pl.BlockSpec((tb, PEER_SEL), lambda i: (i, 0)),
                  pl.BlockSpec(memory_space=pltpu.VMEM),
                  pl.BlockSpec(gsum.shape, lambda i: (0, 0))],
        out_specs=pl.BlockSpec((tb, D_MODEL), lambda i: (i, 0)),
        out_shape=jax.ShapeDtypeStruct((n, D_MODEL), F32),
        scratch_shapes=[pltpu.VMEM((PEER_SEL * HALF_ROWS, LANES), jnp.uint32)] * 2
        + [pltpu.VMEM((PEER_SEL, LANES), F32)] * 2,
        compiler_params=_params(("arbitrary",), VMEM_LIMIT),
        name="peer_dot",
    )(e, xt3, g, tab, gsum)


def _peer_sum_kernel(e_ref, w_ref, h_ref, tab_ref, y_ref, stage0, stage1, *, tb):
    out_rows = 2 * HALF_ROWS
    q = lax.broadcasted_iota(I32, (2 * out_rows, D_MODEL), 0)
    lane = lax.broadcasted_iota(I32, (2 * out_rows, D_MODEL), 1)
    mine = (lane % out_rows == 2 * (q % HALF_ROWS) + q // HALF_ROWS) & (q < out_rows)

    def gather(t, stage):
        _gather_rows(e_ref, tab_ref, t, stage)

    def weighted_sum(t, stage):
        wrow = w_ref[pl.ds(t, 1), :]
        lhs = jnp.where(mine, wrow, 0.0).astype(BF16)
        rows = pltpu.bitcast(stage[...], BF16)
        out = jnp.dot(lhs, rows, preferred_element_type=F32)
        y_ref[t] = h_ref[t] + out[:out_rows]

    stage0[...] = jnp.zeros(stage0.shape, stage0.dtype)
    stage1[...] = jnp.zeros(stage1.shape, stage1.dtype)

    def token_pair(i, carry):
        t0 = 2 * i
        weighted_sum(jnp.maximum(t0 - 2, 0), stage0)
        weighted_sum(jnp.maximum(t0 - 1, 0), stage1)
        gather(t0, stage0)
        gather(t0 + 1, stage1)
        return carry

    lax.fori_loop(0, tb // 2, token_pair, 0)
    weighted_sum(tb - 2, stage0)
    weighted_sum(tb - 1, stage1)


def _peer_sum(e, w, h3, tab, tb=128):
    n = e.shape[0]
    tb = min(tb, n)
    tok = lambda: pl.BlockSpec((tb, 8, LANES), lambda i: (i, 0, 0))
    stage = pltpu.VMEM((PEER_SEL * HALF_ROWS, LANES), jnp.uint32)
    return pl.pallas_call(
        functools.partial(_peer_sum_kernel, tb=tb),
        grid=(n // tb,),
        in_specs=[pl.BlockSpec((tb, PEER_SEL), lambda i: (i, 0), memory_space=pltpu.SMEM),
                  pl.BlockSpec((tb, D_MODEL), lambda i: (i, 0)), tok(),
                  pl.BlockSpec(memory_space=pltpu.VMEM)],
        out_specs=tok(),
        out_shape=jax.ShapeDtypeStruct((n, 8, LANES), F32),
        scratch_shapes=[stage, stage],
        compiler_params=_params(("arbitrary",), VMEM_LIMIT),
        name="peer_sum",
    )(e, w, h3, tab)


def _pack_table(w):
    bits = lax.bitcast_convert_type(w.astype(BF16), jnp.uint16).astype(jnp.uint32)
    half = D_MODEL // 2
    return (bits[:, :half] | (bits[:, half:] << 16)).reshape(w.shape[0] * HALF_ROWS, LANES)


def _block_diag_rect(rows, rgroup, cols, cgroup):
    r = jnp.arange(rows) // rgroup
    c = jnp.arange(cols) // cgroup
    return (r[:, None] == c[None, :]).astype(BF16)


def _state_to_pairs(s):
    b = s.shape[0]
    return s.reshape(b, 2, 2, GLA_DK, GLA_DV).transpose(0, 1, 4, 2, 3).reshape(b, 2, LANES, LANES)


def _pairs_to_state(st):
    b = st.shape[0]
    return st.reshape(b, 2, GLA_DV, 2, GLA_DK).transpose(0, 1, 3, 4, 2).reshape(
        b, GLA_HEADS, GLA_DK, GLA_DV)


def _lambda_init(layer):
    return 0.8 - 0.6 * math.exp(-0.3 * layer)


def _prep_layer(l, w):
    row = lambda a: a.reshape(1, -1).astype(F32)
    w_in = w["w_in"][l]
    lam_init = _lambda_init(l)
    return {
        "attn_norm": row(w["attn_norm"][l]),
        "w_a": w_in[:, :_OFF_GA].astype(BF16),
        "w_ga": jnp.pad(w_in[:, _OFF_GA:_OFF_GR], ((0, 0), (0, LANES - GLA_GATE_RANK))).astype(BF16),
        "w_gr": w_in[:, _OFF_GR:].astype(BF16),
        "dq_norm": row(jnp.tile(w["diff_q_norm"][l], 2 * DIFF_HEADS)) * (DIFF_QK ** -0.5),
        "dk_norm": row(jnp.tile(w["diff_k_norm"][l], 2 * DIFF_HEADS)),
        "bd32": _block_diag(DIFF_W, DIFF_QK, 1.0 / DIFF_QK),
        "w_gate": jnp.pad(w["gla_w_gate"][l], ((0, LANES - GLA_GATE_RANK), (0, 0))).astype(BF16),
        "b_gate": row(w["gla_b_gate"][l]),
        "diff_out_norm": row(jnp.tile(w["diff_out_norm"][l], DIFF_HEADS)) * (1.0 - lam_init),
        "gla_out_norm": row(jnp.tile(w["gla_out_norm"][l], GLA_HEADS)),
        "w_o_d": w["w_o"][l][:DIFF_W].astype(BF16),
        "w_o_g": w["w_o"][l][DIFF_W:].astype(BF16),
        "cross_norm": row(w["cross_norm"][l]),
        "cross_w_q": w["cross_w_q"][l].astype(BF16),
        "cross_q_norm": row(jnp.tile(w["cross_q_norm"][l], CROSS_HEADS)) * (CROSS_HD ** -0.5),
        "mem_norm": row(w["mem_norm"][l]),
        "cross_w_kv": w["cross_w_kv"][l].astype(BF16),
        "cross_k_norm": row(jnp.tile(w["cross_k_norm"][l], CROSS_HEADS)),
        "cross_w_o": w["cross_w_o"][l].astype(BF16),
        "ffn_norm": row(w["ffn_norm"][l]),
        "peer_w_q": w["peer_w_q"][l].astype(BF16),
        "peer_q_norm": row(jnp.tile(w["peer_q_norm"][l], PEER_HEADS)),
        "peer_sub_keys": w["peer_sub_keys"][l].reshape(2 * PEER_HEADS, N_KEYS, N_KEYS).astype(BF16),
        "peer_u": _pack_table(w["peer_u"][l]),
        "peer_v": _pack_table(w["peer_v"][l]),
    }


def _diff_lambda(w, l):
    f = lambda a: a[l].astype(F32)
    lam = (jnp.exp(jnp.sum(f(w["lambda_q1"]) * f(w["lambda_k1"])))
           - jnp.exp(jnp.sum(f(w["lambda_q2"]) * f(w["lambda_k2"]))) + _lambda_init(l))
    return lam.reshape(1, 1)


def _peer(h2, xt, pq, p):
    n = h2.shape[0]
    e, g = _route(pq, p)
    x8 = xt.astype(BF16).reshape(n, 2, HALF_ROWS, LANES).transpose(0, 2, 1, 3).reshape(n, 8, LANES)
    w = _peer_dot(e, jnp.concatenate([x8, x8], axis=1), g, p["peer_u"])
    return _peer_sum(e, w, h2.reshape(n, 8, LANES), p["peer_v"]).reshape(n, D_MODEL)


def _tail(od, og, gr, x, mk, mv, p):
    b, t, _ = x.shape
    n = b * t
    flat = lambda a: a.reshape(n, a.shape[-1])
    h1, qc = _mix_out(flat(od), flat(og), flat(gr), flat(x), p)
    h2, xt, pq = _cross(qc.reshape(b, t, D_MODEL), mk, mv, h1.reshape(b, t, D_MODEL), p)
    return _peer(flat(h2), flat(xt), flat(pq), p).reshape(b, t, D_MODEL)


def kernel(x_prompt, x_sample, mem_prompt, cache_diff_k, cache_diff_v, page_table, state_gla,
           cache_mem_k, cache_mem_v, attn_norm, w_in, diff_q_norm, diff_k_norm, lambda_q1,
           lambda_k1, lambda_q2, lambda_k2, diff_out_norm, gla_w_gate, gla_b_gate, gla_out_norm,
           w_o, cross_norm, mem_norm, cross_w_q, cross_w_kv, cross_q_norm, cross_k_norm,
           cross_w_o, ffn_norm, peer_w_q, peer_q_norm, peer_sub_keys, peer_u, peer_v):
    w = dict(attn_norm=attn_norm, w_in=w_in, diff_q_norm=diff_q_norm, diff_k_norm=diff_k_norm,
             lambda_q1=lambda_q1, lambda_k1=lambda_k1, lambda_q2=lambda_q2, lambda_k2=lambda_k2,
             diff_out_norm=diff_out_norm, gla_w_gate=gla_w_gate, gla_b_gate=gla_b_gate,
             gla_out_norm=gla_out_norm, w_o=w_o, cross_norm=cross_norm, mem_norm=mem_norm,
             cross_w_q=cross_w_q, cross_w_kv=cross_w_kv, cross_q_norm=cross_q_norm,
             cross_k_norm=cross_k_norm, cross_w_o=cross_w_o, ffn_norm=ffn_norm,
             peer_w_q=peer_w_q, peer_q_norm=peer_q_norm, peer_sub_keys=peer_sub_keys,
             peer_u=peer_u, peer_v=peer_v)
    b, s, _ = x_prompt.shape
    db, t, _ = x_sample.shape
    depth = w_in.shape[0]
    hp, hs = x_prompt, x_sample
    outs = [[] for _ in range(8)]
    for l in range(depth):
        p = _prep_layer(l, w)
        lam = _diff_lambda(w, l)

        dq, dk, dv, dk16, dv16, gq, gk, gv, la, gr = _proj_in(hp.reshape(b * s, D_MODEL), p)
        seq = lambda a: a.reshape(b, s, a.shape[-1])
        od = _diff_prompt(lam, seq(dq), seq(dk16), seq(dv16), p)
        st0 = jnp.zeros((b, GLA_HEADS // 2, LANES, LANES), F32)
        og, st = _gla(seq(gq), seq(gk), seq(gv), seq(la), st0, p)
        mk, mv = _mem_kv(mem_prompt.reshape(b * N_MEM, D_MODEL), p)
        mk = mk.reshape(b, N_MEM, D_MODEL)
        mv = mv.reshape(b, N_MEM, D_MODEL)
        hp = _tail(od, og, seq(gr), hp, mk, mv, p)
        outs[0].append(dk.reshape(b, s, DIFF_HEADS, DIFF_V))
        outs[1].append(dv.reshape(b, s, DIFF_HEADS, DIFF_V))
        outs[2].append(_pairs_to_state(st))
        outs[3].append(mk.reshape(b, N_MEM, CROSS_HEADS, CROSS_HD))
        outs[4].append(mv.reshape(b, N_MEM, CROSS_HEADS, CROSS_HD))

        dq, dk, dv, dk16, dv16, gq, gk, gv, la, gr = _proj_in(hs.reshape(db * t, D_MODEL), p)
        seq = lambda a: a.reshape(db, t, a.shape[-1])
        od = _diff_sample(lam, page_table, seq(dq), seq(dk16), seq(dv16), cache_diff_k[l],
                          cache_diff_v[l], p)
        tp = -(-t // GLA_SUB) * GLA_SUB
        padt = lambda a: jnp.pad(seq(a), ((0, 0), (0, tp - t), (0, 0)))
        og, st = _gla(padt(gq), padt(gk), padt(gv), padt(la), _state_to_pairs(state_gla[l]), p)
        hs_pad = jnp.pad(hs, ((0, 0), (0, tp - t), (0, 0)))
        y = _tail(jnp.pad(od, ((0, 0), (0, tp - t), (0, 0))).astype(BF16), og, padt(gr), hs_pad,
                  cache_mem_k[l].reshape(db, N_MEM, D_MODEL),
                  cache_mem_v[l].reshape(db, N_MEM, D_MODEL), p)
        hs = y[:, :t]
        outs[5].append(dk.reshape(db, t, DIFF_HEADS, DIFF_V))
        outs[6].append(dv.reshape(db, t, DIFF_HEADS, DIFF_V))
        outs[7].append(_pairs_to_state(st))

    stack = lambda xs: jnp.stack(xs)
    return (hp, hs, stack(outs[0]), stack(outs[1]), stack(outs[2]), stack(outs[3]), stack(outs[4]),
            stack(outs[5]), stack(outs[6]), stack(outs[7]))
```

```python
import functools
import math

import jax
import jax.numpy as jnp
from jax import lax
from jax.experimental import pallas as pl
from jax.experimental.pallas import tpu as pltpu

F32 = jnp.float32
BF16 = jnp.bfloat16
I32 = jnp.int32

EPS = 1e-6
D_MODEL = 1024
DIFF_HEADS = 8
DIFF_QK = 32
DIFF_V = 64
DIFF_W = DIFF_HEADS * DIFF_V
GLA_HEADS = 4
GLA_DK = 64
GLA_DV = 128
GLA_QW = GLA_HEADS * GLA_DK
GLA_VW = GLA_HEADS * GLA_DV
GLA_GATE_RANK = 16
GLA_TAU = 16.0
GLA_SUB = 16
CROSS_HEADS = 4
CROSS_HD = 256
N_MEM = 256
PEER_HEADS = 8
PEER_QD = 256
N_KEYS = 128
PEER_TOPK = 16
PEER_SEL = PEER_HEADS * PEER_TOPK
PAGE_SIZE = 128
LANES = 128
VMEM_LIMIT = 56 * 1024 * 1024

_OFF_GA = 2560
_OFF_GR = 2576


def _params(sem, vmem=None):
    return pltpu.CompilerParams(dimension_semantics=sem, vmem_limit_bytes=vmem)


def _rms(x, g):
    return x * lax.rsqrt(jnp.mean(x * x, axis=-1, keepdims=True) + EPS) * g


def _split_dot(x, w):
    hi = x.astype(BF16)
    lo = (x - hi.astype(F32)).astype(BF16)
    return (jnp.dot(hi, w, preferred_element_type=F32)
            + jnp.dot(lo, w, preferred_element_type=F32))


def _group_rms(x, bd, g):
    ms = _split_dot(x * x, bd)
    return x * lax.rsqrt(ms + EPS) * g


def _block_diag(width, group, value, dtype=BF16):
    i = jnp.arange(width) // group
    return jnp.where(i[:, None] == i[None, :], value, 0.0).astype(dtype)


def _proj_in_kernel(x_ref, g_ref, wa_ref, wgr_ref, wga_ref, qn_ref, kn_ref, bd_ref, wg_ref, bg_ref,
                    dq_ref, dk_ref, dv_ref, dk16_ref, dv16_ref, gq_ref, gk_ref, gv_ref, la_ref,
                    gr_ref):
    xb = _rms(x_ref[...], g_ref[...]).astype(BF16)
    ya = jnp.dot(xb, wa_ref[...], preferred_element_type=F32)
    bd = bd_ref[...]
    dq = _group_rms(ya[:, 0:512], bd, qn_ref[...])
    dk = _group_rms(ya[:, 512:1024], bd, kn_ref[...])
    dv = ya[:, 1024:1536]
    dq_ref[...] = dq.astype(BF16)
    dk_ref[...] = dk
    dv_ref[...] = dv
    dk16_ref[...] = dk.astype(BF16)
    dv16_ref[...] = dv.astype(BF16)
    gq_ref[...] = ya[:, 1536:1792] * (GLA_DK ** -0.5)
    gk_ref[...] = ya[:, 1792:2048]
    gv_ref[...] = ya[:, 2048:2560]
    gr_ref[...] = jnp.dot(xb, wgr_ref[...], preferred_element_type=F32)
    ga = jnp.dot(xb, wga_ref[...], preferred_element_type=F32)
    z = jnp.dot(ga.astype(BF16), wg_ref[...], preferred_element_type=F32) + bg_ref[...]
    log_sig = jnp.minimum(z, 0.0) - jnp.log(1.0 + jnp.exp(-jnp.abs(z)))
    la_ref[...] = log_sig * (1.0 / GLA_TAU)


def _proj_in(x, p, tm=256):
    n = x.shape[0]
    tm = min(tm, n)
    row = lambda w: pl.BlockSpec((tm, w), lambda i: (i, 0))
    full = lambda a: pl.BlockSpec(a.shape, lambda i: (0,) * a.ndim)
    consts = (p["attn_norm"], p["w_a"], p["w_gr"], p["w_ga"], p["dq_norm"], p["dk_norm"],
              p["bd32"], p["w_gate"], p["b_gate"])
    out_w = ((512, BF16), (512, F32), (512, F32), (512, BF16), (512, BF16), (256, F32), (256, F32),
             (512, F32), (256, F32), (512, F32))
    return pl.pallas_call(
        _proj_in_kernel,
        grid=(n // tm,),
        in_specs=[row(D_MODEL)] + [full(a) for a in consts],
        out_specs=[row(w) for w, _ in out_w],
        out_shape=[jax.ShapeDtypeStruct((n, w), dt) for w, dt in out_w],
        compiler_params=_params(("parallel",), VMEM_LIMIT),
        name="proj_in",
    )(x, *consts)


def _mix_out_kernel(od_ref, og_ref, gr_ref, x_ref, gn_ref, wod_ref, wog_ref, cn_ref, wq_ref,
                    qn_ref, h_ref, qc_ref):
    og = og_ref[...]
    gr = gr_ref[...]
    gn = gn_ref[...]
    parts = []
    for h in range(GLA_HEADS):
        sl = slice(h * GLA_DV, (h + 1) * GLA_DV)
        parts.append(_rms(og[:, sl], gn[:, sl]))
    ogn = jnp.concatenate(parts, axis=-1) * (gr * jax.nn.sigmoid(gr))
    mix = (jnp.dot(od_ref[...], wod_ref[...], preferred_element_type=F32)
           + jnp.dot(ogn.astype(BF16), wog_ref[...], preferred_element_type=F32))
    h1 = x_ref[...] + mix
    h_ref[...] = h1
    hn = _rms(h1, cn_ref[...]).astype(BF16)
    q = jnp.dot(hn, wq_ref[...], preferred_element_type=F32)
    qn = qn_ref[...]
    parts = []
    for h in range(CROSS_HEADS):
        sl = slice(h * CROSS_HD, (h + 1) * CROSS_HD)
        parts.append(_rms(q[:, sl], qn[:, sl]))
    qc_ref[...] = jnp.concatenate(parts, axis=-1).astype(BF16)


def _mix_out(od, og, gr, x, p, tm=256):
    n = x.shape[0]
    tm = min(tm, n)
    row = lambda w: pl.BlockSpec((tm, w), lambda i: (i, 0))
    full = lambda a: pl.BlockSpec(a.shape, lambda i: (0,) * a.ndim)
    consts = (p["gla_out_norm"], p["w_o_d"], p["w_o_g"], p["cross_norm"], p["cross_w_q"],
              p["cross_q_norm"])
    return pl.pallas_call(
        _mix_out_kernel,
        grid=(n // tm,),
        in_specs=[row(512), row(512), row(512), row(D_MODEL)] + [full(a) for a in consts],
        out_specs=[row(D_MODEL), row(D_MODEL)],
        out_shape=[jax.ShapeDtypeStruct((n, D_MODEL), F32), jax.ShapeDtypeStruct((n, D_MODEL), BF16)],
        compiler_params=_params(("parallel",), VMEM_LIMIT),
        name="mix_out",
    )(od, og, gr, x, *consts)


def _mem_kv_kernel(m_ref, mn_ref, wkv_ref, kn_ref, k_ref, v_ref):
    mb = _rms(m_ref[...], mn_ref[...]).astype(BF16)
    kv = jnp.dot(mb, wkv_ref[...], preferred_element_type=F32)
    kn = kn_ref[...]
    parts = []
    for h in range(CROSS_HEADS):
        sl = slice(h * CROSS_HD, (h + 1) * CROSS_HD)
        parts.append(_rms(kv[:, sl], kn[:, sl]))
    k_ref[...] = jnp.concatenate(parts, axis=-1)
    v_ref[...] = kv[:, D_MODEL:]


def _mem_kv(mem, p, tm=256):
    n = mem.shape[0]
    tm = min(tm, n)
    row = lambda w: pl.BlockSpec((tm, w), lambda i: (i, 0))
    full = lambda a: pl.BlockSpec(a.shape, lambda i: (0,) * a.ndim)
    consts = (p["mem_norm"], p["cross_w_kv"], p["cross_k_norm"])
    return pl.pallas_call(
        _mem_kv_kernel,
        grid=(n // tm,),
        in_specs=[row(D_MODEL)] + [full(a) for a in consts],
        out_specs=[row(D_MODEL), row(D_MODEL)],
        out_shape=[jax.ShapeDtypeStruct((n, D_MODEL), F32)] * 2,
        compiler_params=_params(("parallel",), VMEM_LIMIT),
        name="mem_kv",
    )(mem, *consts)


def _cross_kernel(qc_ref, mk_ref, mv_ref, h1_ref, wo_ref, fn_ref, wpq_ref, pqn_ref,
                  h2_ref, xt_ref, pq_ref):
    qc = qc_ref[0]
    mk = mk_ref[0].astype(BF16)
    mv = mv_ref[0].astype(BF16)
    parts = []
    for h in range(CROSS_HEADS):
        sl = slice(h * CROSS_HD, (h + 1) * CROSS_HD)
        s = lax.dot_general(qc[:, sl], mk[:, sl], (((1,), (1,)), ((), ())),
                            preferred_element_type=F32)
        e = jnp.exp(s - jnp.max(s, axis=-1, keepdims=True))
        prob = e / jnp.sum(e, axis=-1, keepdims=True)
        parts.append(jnp.dot(prob.astype(BF16), mv[:, sl], preferred_element_type=F32))
    o = jnp.concatenate(parts, axis=-1).astype(BF16)
    h2 = h1_ref[0] + jnp.dot(o, wo_ref[...], preferred_element_type=F32)
    h2_ref[0] = h2
    xt = _rms(h2, fn_ref[...])
    xt_ref[0] = xt
    pq = jnp.dot(xt.astype(BF16), wpq_ref[...], preferred_element_type=F32)
    pqn = pqn_ref[...]
    parts = []
    for h in range(PEER_HEADS):
        sl = slice(h * PEER_QD, (h + 1) * PEER_QD)
        parts.append(_rms(pq[:, sl], pqn[:, sl]))
    pq_ref[0] = jnp.concatenate(parts, axis=-1).astype(BF16)


def _cross(qc, mk, mv, h1, p, tq=256):
    b, t, _ = qc.shape
    tq = min(tq, t)
    row = lambda w: pl.BlockSpec((1, tq, w), lambda i, j: (i, j, 0))
    mem = pl.BlockSpec((1, N_MEM, D_MODEL), lambda i, j: (i, 0, 0))
    full = lambda a: pl.BlockSpec(a.shape, lambda i, j: (0,) * a.ndim)
    consts = (p["cross_w_o"], p["ffn_norm"], p["peer_w_q"], p["peer_q_norm"])
    return pl.pallas_call(
        _cross_kernel,
        grid=(b, t // tq),
        in_specs=[row(D_MODEL), mem, mem, row(D_MODEL)] + [full(a) for a in consts],
        out_specs=[row(D_MODEL), row(D_MODEL), row(2048)],
        out_shape=[jax.ShapeDtypeStruct((b, t, D_MODEL), F32),
                   jax.ShapeDtypeStruct((b, t, D_MODEL), F32),
                   jax.ShapeDtypeStruct((b, t, 2048), BF16)],
        compiler_params=_params(("parallel", "parallel"), VMEM_LIMIT),
        name="cross_attn",
    )(qc, mk, mv, h1, *consts)


def _diff_prompt_kernel(lam_ref, qt_ref, k_ref, vt_ref, on_ref, o_ref, m_scr, l_scr, acc_scr, *, tq):
    qi = pl.program_id(1)
    lam = lam_ref[0, 0]
    cols = 4 * tq
    chan = lax.broadcasted_iota(I32, (LANES, cols), 0)
    my_map = chan // DIFF_QK == lax.broadcasted_iota(I32, (LANES, cols), 1) // tq
    causal = (lax.broadcasted_iota(I32, (tq, cols), 0)
              <= lax.broadcasted_iota(I32, (tq, cols), 1) % tq)
    top_head = lax.broadcasted_iota(I32, (LANES, tq), 0) < DIFF_V
    for p in range(DIFF_HEADS // 2):
        sl = slice(p * LANES, (p + 1) * LANES)
        qtp = qt_ref[0, sl, :]
        qs = jnp.where(my_map, jnp.concatenate([qtp] * 4, axis=1), jnp.zeros((LANES, cols), BF16))
        m_scr[...] = jnp.full((1, cols), -jnp.inf, F32)
        l_scr[...] = jnp.zeros((1, cols), F32)
        acc_scr[...] = jnp.zeros((LANES, cols), F32)

        def block(kb, masked, qs=qs):
            t0 = pl.multiple_of(kb * tq, tq)
            s = jnp.dot(k_ref[0, pl.ds(t0, tq), sl], qs, preferred_element_type=F32)
            if masked:
                s = jnp.where(causal, s, -jnp.inf)
            m_old = m_scr[...]
            m_new = jnp.maximum(m_old, jnp.max(s, axis=0, keepdims=True))
            alpha = jnp.exp(m_old - m_new)
            e = jnp.exp(s - m_new)
            m_scr[...] = m_new
            l_scr[...] = alpha * l_scr[...] + jnp.sum(e, axis=0, keepdims=True)
            acc_scr[...] = alpha * acc_scr[...] + jnp.dot(
                vt_ref[0, sl, pl.ds(t0, tq)], e.astype(BF16), preferred_element_type=F32)

        def body(kb, carry):
            block(kb, False)
            return carry

        lax.fori_loop(0, qi, body, 0)
        block(qi, True)
        on = acc_scr[...] / l_scr[...]
        ot = jnp.where(top_head, on[:, :tq] - lam * on[:, tq:2 * tq],
                       on[:, 2 * tq:3 * tq] - lam * on[:, 3 * tq:])
        sq = ot * ot
        ms = jnp.where(top_head, jnp.sum(sq[:DIFF_V], axis=0, keepdims=True),
                       jnp.sum(sq[DIFF_V:], axis=0, keepdims=True)) * (1.0 / DIFF_V)
        ot = ot * lax.rsqrt(ms + EPS) * on_ref[sl, :]
        o_ref[0, :, sl] = ot.T.astype(BF16)


def _diff_prompt(lam, dq, dk16, dv16, p, tq=256):
    b, s, _ = dq.shape
    tq = min(tq, s)
    qt = jnp.swapaxes(dq, 1, 2)
    vt = jnp.swapaxes(dv16, 1, 2)
    return pl.pallas_call(
        functools.partial(_diff_prompt_kernel, tq=tq),
        grid=(b, s // tq),
        in_specs=[pl.BlockSpec(memory_space=pltpu.SMEM),
                  pl.BlockSpec((1, DIFF_W, tq), lambda i, j: (i, 0, j)),
                  pl.BlockSpec((1, s, DIFF_W), lambda i, j: (i, 0, 0)),
                  pl.BlockSpec((1, DIFF_W, s), lambda i, j: (i, 0, 0)),
                  pl.BlockSpec((DIFF_W, 1), lambda i, j: (0, 0))],
        out_specs=pl.BlockSpec((1, tq, DIFF_W), lambda i, j: (i, j, 0)),
        out_shape=jax.ShapeDtypeStruct((b, s, DIFF_W), BF16),
        scratch_shapes=[pltpu.VMEM((1, 4 * tq), F32), pltpu.VMEM((1, 4 * tq), F32),
                        pltpu.VMEM((LANES, 4 * tq), F32)],
        compiler_params=_params(("parallel", "parallel"), VMEM_LIMIT),
        name="diff_prompt",
    )(lam, qt, dk16, vt, p["diff_out_norm"].reshape(DIFF_W, 1))


def _gla_kernel(q_ref, k_ref, v_ref, la_ref, st0_ref, tri_ref, ones_ref, o_ref, stout_ref,
                st_scr, b_scr, *, ts):
    si = pl.program_id(1)

    @pl.when(si == 0)
    def _():
        st_scr[...] = st0_ref[0]

    la = la_ref[0]
    hi = la.astype(BF16)
    r1 = la - hi.astype(F32)
    mid = r1.astype(BF16)
    lo = (r1 - mid.astype(F32)).astype(BF16)
    tri = tri_ref[...]
    b = (jnp.dot(tri, hi, preferred_element_type=F32) + jnp.dot(tri, mid, preferred_element_type=F32)
         + jnp.dot(tri, lo, preferred_element_type=F32))
    b_scr[...] = b

    q = q_ref[0]
    k = k_ref[0]
    v = v_ref[0]
    sub_q = lax.broadcasted_iota(I32, (ts, GLA_QW), 0) % GLA_SUB
    sub_v = lax.broadcasted_iota(I32, (ts, GLA_VW), 0) % GLA_SUB
    ones = ones_ref[...]
    o_intra = jnp.zeros((ts, GLA_VW), F32)
    for d in range(GLA_SUB):
        if d == 0:
            kd, bd, vd = k, b, v
        else:
            kd = pltpu.roll(k, d, 0)
            bd = pltpu.roll(b, d, 0)
            vd = pltpu.roll(v, d, 0)
        ok = sub_q >= d
        e = jnp.where(ok, q * kd * jnp.exp(jnp.where(ok, b - bd, 0.0)), 0.0)
        w = jnp.dot(e.astype(BF16), ones, preferred_element_type=F32)
        o_intra = o_intra + w * jnp.where(sub_v >= d, vd, 0.0)
    o_ref[0] = o_intra

    lane = lax.broadcasted_iota(I32, (GLA_SUB, LANES), 1)
    lane_sq = lax.broadcasted_iota(I32, (LANES, LANES), 1)

    def step(c, carry):
        t0 = pl.multiple_of(c * GLA_SUB, GLA_SUB)
        bc = b_scr[pl.ds(t0, GLA_SUB), :]
        qc = q_ref[0, pl.ds(t0, GLA_SUB), :]
        kc = k_ref[0, pl.ds(t0, GLA_SUB), :]
        vc = v_ref[0, pl.ds(t0, GLA_SUB), :]
        bl = bc[GLA_SUB - 1:GLA_SUB, :]
        qe = qc * jnp.exp(bc)
        ke = (kc * jnp.exp(bl - bc)).astype(BF16)
        dec = jnp.exp(bl)
        o_parts = []
        for p in range(GLA_HEADS // 2):
            sl = slice(p * LANES, (p + 1) * LANES)
            st = st_scr[p]
            stb = st.astype(BF16)
            qp = qe[:, sl]
            upd = []
            for hh in range(2):
                mine = (lane < GLA_DK) if hh == 0 else (lane >= GLA_DK)
                qm = jnp.where(mine, qp, 0.0).astype(BF16)
                o_parts.append(lax.dot_general(qm, stb, (((1,), (1,)), ((), ())),
                                               preferred_element_type=F32))
                vh = vc[:, (2 * p + hh) * GLA_DV:(2 * p + hh + 1) * GLA_DV]
                upd.append(jnp.dot(vh.T.astype(BF16), ke[:, sl], preferred_element_type=F32))
            st_scr[p] = dec[:, sl] * st + jnp.where(lane_sq < GLA_DK, upd[0], upd[1])
        o_ref[0, pl.ds(t0, GLA_SUB), :] += jnp.concatenate(o_parts, axis=-1)
        return carry

    lax.fori_loop(0, ts // GLA_SUB, step, 0)

    @pl.when(si == pl.num_programs(1) - 1)
    def _():
        stout_ref[0] = st_scr[...]


def _gla(gq, gk, gv, la, st0, p, ts=256):
    b, t, _ = gq.shape
    ts = min(ts, t)
    i = jnp.arange(ts)
    tri = ((i[:, None] // GLA_SUB == i[None, :] // GLA_SUB) & (i[None, :] <= i[:, None])).astype(BF16)
    ones = _block_diag_rect(GLA_QW, GLA_DK, GLA_VW, GLA_DV)
    seq = lambda w: pl.BlockSpec((1, ts, w), lambda bi, si: (bi, si, 0))
    stspec = pl.BlockSpec((1, GLA_HEADS // 2, LANES, LANES), lambda bi, si: (bi, 0, 0, 0))
    return pl.pallas_call(
        functools.partial(_gla_kernel, ts=ts),
        grid=(b, t // ts),
        in_specs=[seq(GLA_QW), seq(GLA_QW), seq(GLA_VW), seq(GLA_QW), stspec,
                  pl.BlockSpec((ts, ts), lambda bi, si: (0, 0)),
                  pl.BlockSpec((GLA_QW, GLA_VW), lambda bi, si: (0, 0))],
        out_specs=[seq(GLA_VW), stspec],
        out_shape=[jax.ShapeDtypeStruct((b, t, GLA_VW), F32),
                   jax.ShapeDtypeStruct((b, GLA_HEADS // 2, LANES, LANES), F32)],
        scratch_shapes=[pltpu.VMEM((GLA_HEADS // 2, LANES, LANES), F32),
                        pltpu.VMEM((ts, GLA_QW), F32)],
        compiler_params=_params(("parallel", "arbitrary"), VMEM_LIMIT),
        name="gla",
    )(gq, gk, gv, la, st0, tri, ones)


PAGES_PER_STEP = 8
NEW_T = 8


def _diff_sample_kernel(pt_ref, lam_ref, q_ref, *rest):
    del pt_ref
    np_ = PAGES_PER_STEP
    k_refs = rest[:np_]
    v_refs = rest[np_:2 * np_]
    kn_ref, vn_ref, on_ref, o_ref, m_scr, l_scr, acc_scr = rest[2 * np_:]
    i = pl.program_id(1)
    nt = (((1,), (1,)), ((), ()))

    @pl.when(i == 0)
    def _():
        m_scr[...] = jnp.full(m_scr.shape, -jnp.inf, F32)
        l_scr[...] = jnp.zeros(l_scr.shape, F32)
        acc_scr[...] = jnp.zeros(acc_scr.shape, F32)

    q = q_ref[0]

    def update(scores, value_products):
        m_old = m_scr[...]
        m_new = m_old
        for s in scores:
            m_new = jnp.maximum(m_new, jnp.max(s, axis=-1, keepdims=True))
        alpha = jnp.exp(m_old - m_new)
        l = alpha * l_scr[...]
        acc = alpha * acc_scr[...]
        for s, pv in zip(scores, value_products):
            e = jnp.exp(s - m_new)
            l = l + jnp.sum(e, axis=-1, keepdims=True)
            acc = acc + pv(e.astype(BF16))
        m_scr[...] = m_new
        l_scr[...] = l
        acc_scr[...] = acc

    def page_t(ref):
        return ref[...].reshape(DIFF_W, PAGE_SIZE).astype(BF16)

    scores = [jnp.dot(q, page_t(k_ref), preferred_element_type=F32) for k_ref in k_refs]
    update(scores, [lambda e, v_ref=v_ref: lax.dot_general(e, page_t(v_ref), nt,
                                                           preferred_element_type=F32)
                    for v_ref in v_refs])

    @pl.when(i == pl.num_programs(1) - 1)
    def _():
        s = lax.dot_general(q, kn_ref[0], nt, preferred_element_type=F32)
        t_q = lax.broadcasted_iota(I32, s.shape, 0) % NEW_T
        t_k = lax.broadcasted_iota(I32, s.shape, 1)
        update([jnp.where(t_k <= t_q, s, -jnp.inf)],
               [lambda e: jnp.dot(e, vn_ref[0], preferred_element_type=F32)])
        accn = acc_scr[...] / l_scr[...]
        rows = 2 * NEW_T
        lane = lax.broadcasted_iota(I32, (rows, DIFF_W), 1)
        o16 = jnp.zeros((rows, DIFF_W), F32)
        for h in range(DIFF_HEADS):
            o16 = jnp.where(lane // DIFF_V == h, accn[h * rows:(h + 1) * rows, :], o16)
        o = o16[:NEW_T] - lam_ref[0, 0] * o16[NEW_T:]
        lane = lax.broadcasted_iota(I32, (NEW_T, DIFF_W), 1)
        sq = o * o
        ms = jnp.zeros_like(o)
        for h in range(DIFF_HEADS):
            mine = lane // DIFF_V == h
            ms = jnp.where(mine, jnp.sum(jnp.where(mine, sq, 0.0), axis=-1, keepdims=True), ms)
        o_ref[0] = o * lax.rsqrt(ms * (1.0 / DIFF_V) + EPS) * on_ref[...]


def _diff_sample(lam, page_table, dq, dk16, dv16, cache_k, cache_v, p):
    db, t, _ = dq.shape
    n_pages = page_table.shape[1]
    np_ = PAGES_PER_STEP
    a = dq.reshape(db, t, 16, DIFF_QK).transpose(0, 2, 1, 3)
    q_bd = (a[:, :, :, None, :] * jnp.eye(16, dtype=BF16)[None, :, None, :, None]
            ).reshape(db, 16 * t, DIFF_W)
    pad = lambda x: jnp.pad(x, ((0, 0), (0, LANES - t), (0, 0)))
    ck = cache_k.transpose(0, 2, 3, 1)
    cv = cache_v.transpose(0, 2, 3, 1)

    def page_spec(n):
        return pl.BlockSpec((None, DIFF_HEADS, DIFF_V, PAGE_SIZE),
                            lambda b, i, pt: (pt[b, i * np_ + n], 0, 0, 0))

    per_b = lambda r: pl.BlockSpec((1, r, DIFF_W), lambda b, i, pt: (b, 0, 0))
    grid_spec = pltpu.PrefetchScalarGridSpec(
        num_scalar_prefetch=1,
        grid=(db, n_pages // np_),
        in_specs=[pl.BlockSpec(memory_space=pltpu.SMEM), per_b(16 * t)]
        + [page_spec(n) for n in range(np_)] + [page_spec(n) for n in range(np_)]
        + [per_b(LANES), per_b(LANES), pl.BlockSpec((1, DIFF_W), lambda b, i, pt: (0, 0))],
        out_specs=per_b(t),
        scratch_shapes=[pltpu.VMEM((16 * t, 1), F32), pltpu.VMEM((16 * t, 1), F32),
                        pltpu.VMEM((16 * t, DIFF_W), F32)],
    )
    return pl.pallas_call(
        _diff_sample_kernel,
        grid_spec=grid_spec,
        out_shape=jax.ShapeDtypeStruct((db, t, DIFF_W), F32),
        compiler_params=_params(("parallel", "arbitrary"), VMEM_LIMIT),
        name="diff_sample",
    )(page_table, lam, q_bd, *([ck] * np_), *([cv] * np_), pad(dk16), pad(dv16),
      p["diff_out_norm"])


def _top16_rows(s, row):
    vals, idxs = [], []
    sentinel = s.shape[0]
    for _ in range(PEER_TOPK):
        m = jnp.max(s, axis=0, keepdims=True)
        idx = jnp.min(jnp.where(s == m, row, sentinel), axis=0, keepdims=True)
        vals.append(m)
        idxs.append(idx)
        s = jnp.where(row == idx, -jnp.inf, s)
    return jnp.concatenate(vals, axis=0), jnp.concatenate(idxs, axis=0)


def _pick_row(table, row, idx):
    return jnp.max(jnp.where(row == idx, table, -1), axis=0, keepdims=True)


_CAND_WIDTH = [PEER_TOPK // (a + 1) for a in range(PEER_TOPK)]
_CAND_ROWS = -(-sum(_CAND_WIDTH) // 8) * 8


def _route_kernel(pq_ref, sk_ref, e_ref, g_ref, *, tb):
    nt = (((1,), (1,)), ((), ()))
    row_k = lax.broadcasted_iota(I32, (N_KEYS, tb), 0)
    row_c = lax.broadcasted_iota(I32, (_CAND_ROWS, tb), 0)
    n_pad = _CAND_ROWS - sum(_CAND_WIDTH)
    e_rows, g_rows = [], []
    for h in range(PEER_HEADS):
        tops = []
        for c in range(2):
            hc = 2 * h + c
            s = lax.dot_general(sk_ref[hc], pq_ref[:, hc * N_KEYS:(hc + 1) * N_KEYS], nt,
                                preferred_element_type=F32)
            tops.append(_top16_rows(s, row_k))
        (s1, i1), (s2, i2) = tops
        cand = jnp.concatenate(
            [jnp.broadcast_to(s1[a:a + 1], (nb, tb)) + s2[:nb] for a, nb in enumerate(_CAND_WIDTH)]
            + [jnp.full((n_pad, tb), -jnp.inf, F32)], axis=0)
        cand_e = jnp.concatenate(
            [jnp.broadcast_to(i1[a:a + 1], (nb, tb)) * N_KEYS + i2[:nb]
             for a, nb in enumerate(_CAND_WIDTH)] + [jnp.zeros((n_pad, tb), I32)], axis=0)
        top_s, j = _top16_rows(cand, row_c)
        e_rows.append(jnp.concatenate(
            [_pick_row(cand_e, row_c, j[r:r + 1]) for r in range(PEER_TOPK)], axis=0))
        ex = jnp.exp(top_s - top_s[0:1])
        g_rows.append(ex / jnp.sum(ex, axis=0, keepdims=True))
    e_ref[...] = (jnp.concatenate(e_rows, axis=0) * HALF_ROWS).T
    g_ref[...] = jnp.concatenate(g_rows, axis=0).T


def _route(pq, p, tb=128):
    n = pq.shape[0]
    tb = min(tb, n)
    return pl.pallas_call(
        functools.partial(_route_kernel, tb=tb),
        grid=(n // tb,),
        in_specs=[pl.BlockSpec((tb, 2048), lambda i: (i, 0)),
                  pl.BlockSpec((2 * PEER_HEADS, N_KEYS, N_KEYS), lambda i: (0, 0, 0))],
        out_specs=[pl.BlockSpec((tb, PEER_SEL), lambda i: (i, 0))] * 2,
        out_shape=[jax.ShapeDtypeStruct((n, PEER_SEL), I32), jax.ShapeDtypeStruct((n, PEER_SEL), F32)],
        compiler_params=_params(("parallel",), VMEM_LIMIT),
        name="peer_route",
    )(pq, p["peer_sub_keys"])


HALF_ROWS = 4


def _table_row(tab_ref, row_start):
    return tab_ref[pl.ds(pl.multiple_of(row_start, HALF_ROWS), HALF_ROWS), :]


def _unpack_pair(row):
    lo = lax.bitcast_convert_type(row << 16, F32)
    hi = lax.bitcast_convert_type(row & jnp.uint32(0xFFFF0000), F32)
    return lo, hi


def _gelu_tanh(x):
    return 0.5 * x * (1.0 + jnp.tanh(math.sqrt(2.0 / math.pi) * (x + 0.044715 * (x * x * x))))


def _gather_rows(e_ref, tab_ref, t, stage):
    for j in range(PEER_SEL):
        stage[j * HALF_ROWS:(j + 1) * HALF_ROWS, :] = _table_row(tab_ref, e_ref[t, j])


def _peer_dot_kernel(e_ref, xt_ref, g_ref, tab_ref, gsum_ref, w_ref, stage0, stage1,
                     part0, part1, *, tb):
    lane = lax.broadcasted_iota(I32, (PEER_SEL, tb), 1)
    gsum = gsum_ref[...]
    chunks = 2 * HALF_ROWS
    gsum16 = gsum_ref[:LANES // chunks, :LANES]
    pairs = PEER_SEL * chunks // 16

    def group_sums(t):
        prods = []
        for k, stage in enumerate((stage0, stage1)):
            x = xt_ref[jnp.maximum(t + k, 0)]
            rows = pltpu.bitcast(stage[...], BF16).reshape(pairs, 16, LANES)
            prods.append((rows * x[None]).reshape(PEER_SEL * chunks, LANES))
        per = LANES // chunks
        for c in range(PEER_SEL // per):
            for prod, part in zip(prods, (part0, part1)):
                part[c * per:(c + 1) * per, :] = jnp.dot(
                    gsum16, prod[c * LANES:(c + 1) * LANES, :], preferred_element_type=F32)

    def lane_sums(t, part, ht):
        return jnp.where(lane == t, jnp.sum(part[...], axis=1, keepdims=True), ht)

    for ref in (stage0, stage1, part0, part1):
        ref[...] = jnp.zeros(ref.shape, ref.dtype)

    def token_pair(i, ht):
        t0 = 2 * i
        ht = lane_sums(t0 - 4, part0, ht)
        ht = lane_sums(t0 - 3, part1, ht)
        group_sums(t0 - 2)
        _gather_rows(e_ref, tab_ref, t0, stage0)
        _gather_rows(e_ref, tab_ref, t0 + 1, stage1)
        return ht

    ht = lax.fori_loop(0, tb // 2, token_pair, jnp.zeros((PEER_SEL, tb), F32))
    ht = lane_sums(tb - 4, part0, ht)
    ht = lane_sums(tb - 3, part1, ht)
    group_sums(tb - 2)
    ht = lane_sums(tb - 2, part0, ht)
    ht = lane_sums(tb - 1, part1, ht)
    w = g_ref[...] * _gelu_tanh(ht.T)
    w_ref[...] = jnp.dot(w.astype(BF16), gsum, preferred_element_type=F32)


def _peer_dot(e, xt3, g, tab, tb=128):
    n = e.shape[0]
    tb = min(tb, n)
    gsum = _block_diag_rect(PEER_SEL, 1, PEER_SEL * 2 * HALF_ROWS, 2 * HALF_ROWS)
    return pl.pallas_call(
        functools.partial(_peer_dot_kernel, tb=tb),
        grid=(n // tb,),
        in_specs=[pl.BlockSpec((tb, PEER_SEL), lambda i: (i, 0), memory_space=pltpu.SMEM),
                  pl.BlockSpec((tb, 16, LANES), lambda i: (i, 0, 0)),
                  pl.BlockSpec((tb, PEER_SEL), lambda i: (i, 0)),
                  pl.BlockSpec(memory_space=pltpu.VMEM),
                  pl.BlockSpec(gsum.shape, lambda i: (0, 0))],
        out_specs=pl.BlockSpec((tb, D_MODEL), lambda i: (i, 0)),
        out_shape=jax.ShapeDtypeStruct((n, D_MODEL), F32),
        scratch_shapes=[pltpu.VMEM((PEER_SEL * HALF_ROWS, LANES), jnp.uint32)] * 2
        + [pltpu.VMEM((PEER_SEL, LANES), F32)] * 2,
        compiler_params=_params(("arbitrary",), VMEM_LIMIT),
        name="peer_dot",
    )(e, xt3, g, tab, gsum)


def _peer_sum_kernel(e_ref, w_ref, h_ref, tab_ref, y_ref, stage0, stage1, *, tb):
    out_rows = 2 * HALF_ROWS
    q = lax.broadcasted_iota(I32, (2 * out_rows, D_MODEL), 0)
    lane = lax.broadcasted_iota(I32, (2 * out_rows, D_MODEL), 1)
    mine = (lane % out_rows == 2 * (q % HALF_ROWS) + q // HALF_ROWS) & (q < out_rows)

    def gather(t, stage):
        _gather_rows(e_ref, tab_ref, t, stage)

    def weighted_sum(t, stage):
        wrow = w_ref[pl.ds(t, 1), :]
        lhs = jnp.where(mine, wrow, 0.0).astype(BF16)
        rows = pltpu.bitcast(stage[...], BF16)
        out = jnp.dot(lhs, rows, preferred_element_type=F32)
        y_ref[t] = h_ref[t] + out[:out_rows]

    stage0[...] = jnp.zeros(stage0.shape, stage0.dtype)
    stage1[...] = jnp.zeros(stage1.shape, stage1.dtype)

    def token_pair(i, carry):
        t0 = 2 * i
        weighted_sum(jnp.maximum(t0 - 2, 0), stage0)
        weighted_sum(jnp.maximum(t0 - 1, 0), stage1)
        gather(t0, stage0)
        gather(t0 + 1, stage1)
        return carry

    lax.fori_loop(0, tb // 2, token_pair, 0)
    weighted_sum(tb - 2, stage0)
    weighted_sum(tb - 1, stage1)


def _peer_sum(e, w, h3, tab, tb=128):
    n = e.shape[0]
    tb = min(tb, n)
    tok = lambda: pl.BlockSpec((tb, 8, LANES), lambda i: (i, 0, 0))
    stage = pltpu.VMEM((PEER_SEL * HALF_ROWS, LANES), jnp.uint32)
    return pl.pallas_call(
        functools.partial(_peer_sum_kernel, tb=tb),
        grid=(n // tb,),
        in_specs=[pl.BlockSpec((tb, PEER_SEL), lambda i: (i, 0), memory_space=pltpu.SMEM),
                  pl.BlockSpec((tb, D_MODEL), lambda i: (i, 0)), tok(),
                  pl.BlockSpec(memory_space=pltpu.VMEM)],
        out_specs=tok(),
        out_shape=jax.ShapeDtypeStruct((n, 8, LANES), F32),
        scratch_shapes=[stage, stage],
        compiler_params=_params(("arbitrary",), VMEM_LIMIT),
        name="peer_sum",
    )(e, w, h3, tab)


def _pack_table(w):
    bits = lax.bitcast_convert_type(w.astype(BF16), jnp.uint16).astype(jnp.uint32)
    half = D_MODEL // 2
    return (bits[:, :half] | (bits[:, half:] << 16)).reshape(w.shape[0] * HALF_ROWS, LANES)


def _block_diag_rect(rows, rgroup, cols, cgroup):
    r = jnp.arange(rows) // rgroup
    c = jnp.arange(cols) // cgroup
    return (r[:, None] == c[None, :]).astype(BF16)


def _state_to_pairs(s):
    b = s.shape[0]
    return s.reshape(b, 2, 2, GLA_DK, GLA_DV).transpose(0, 1, 4, 2, 3).reshape(b, 2, LANES, LANES)


def _pairs_to_state(st):
    b = st.shape[0]
    return st.reshape(b, 2, GLA_DV, 2, GLA_DK).transpose(0, 1, 3, 4, 2).reshape(
        b, GLA_HEADS, GLA_DK, GLA_DV)


def _lambda_init(layer):
    return 0.8 - 0.6 * math.exp(-0.3 * layer)


def _prep_layer(l, w):
    row = lambda a: a.reshape(1, -1).astype(F32)
    w_in = w["w_in"][l]
    lam_init = _lambda_init(l)
    return {
        "attn_norm": row(w["attn_norm"][l]),
        "w_a": w_in[:, :_OFF_GA].astype(BF16),
        "w_ga": jnp.pad(w_in[:, _OFF_GA:_OFF_GR], ((0, 0), (0, LANES - GLA_GATE_RANK))).astype(BF16),
        "w_gr": w_in[:, _OFF_GR:].astype(BF16),
        "dq_norm": row(jnp.tile(w["diff_q_norm"][l], 2 * DIFF_HEADS)) * (DIFF_QK ** -0.5),
        "dk_norm": row(jnp.tile(w["diff_k_norm"][l], 2 * DIFF_HEADS)),
        "bd32": _block_diag(DIFF_W, DIFF_QK, 1.0 / DIFF_QK),
        "w_gate": jnp.pad(w["gla_w_gate"][l], ((0, LANES - GLA_GATE_RANK), (0, 0))).astype(BF16),
        "b_gate": row(w["gla_b_gate"][l]),
        "diff_out_norm": row(jnp.tile(w["diff_out_norm"][l], DIFF_HEADS)) * (1.0 - lam_init),
        "gla_out_norm": row(jnp.tile(w["gla_out_norm"][l], GLA_HEADS)),
        "w_o_d": w["w_o"][l][:DIFF_W].astype(BF16),
        "w_o_g": w["w_o"][l][DIFF_W:].astype(BF16),
        "cross_norm": row(w["cross_norm"][l]),
        "cross_w_q": w["cross_w_q"][l].astype(BF16),
        "cross_q_norm": row(jnp.tile(w["cross_q_norm"][l], CROSS_HEADS)) * (CROSS_HD ** -0.5),
        "mem_norm": row(w["mem_norm"][l]),
        "cross_w_kv": w["cross_w_kv"][l].astype(BF16),
        "cross_k_norm": row(jnp.tile(w["cross_k_norm"][l], CROSS_HEADS)),
        "cross_w_o": w["cross_w_o"][l].astype(BF16),
        "ffn_norm": row(w["ffn_norm"][l]),
        "peer_w_q": w["peer_w_q"][l].astype(BF16),
        "peer_q_norm": row(jnp.tile(w["peer_q_norm"][l], PEER_HEADS)),
        "peer_sub_keys": w["peer_sub_keys"][l].reshape(2 * PEER_HEADS, N_KEYS, N_KEYS).astype(BF16),
        "peer_u": _pack_table(w["peer_u"][l]),
        "peer_v": _pack_table(w["peer_v"][l]),
    }


def _diff_lambda(w, l):
    f = lambda a: a[l].astype(F32)
    lam = (jnp.exp(jnp.sum(f(w["lambda_q1"]) * f(w["lambda_k1"])))
           - jnp.exp(jnp.sum(f(w["lambda_q2"]) * f(w["lambda_k2"]))) + _lambda_init(l))
    return lam.reshape(1, 1)


def _peer(h2, xt, pq, p):
    n = h2.shape[0]
    e, g = _route(pq, p)
    x8 = xt.astype(BF16).reshape(n, 2, HALF_ROWS, LANES).transpose(0, 2, 1, 3).reshape(n, 8, LANES)
    w = _peer_dot(e, jnp.concatenate([x8, x8], axis=1), g, p["peer_u"])
    return _peer_sum(e, w, h2.reshape(n, 8, LANES), p["peer_v"]).reshape(n, D_MODEL)


def _tail(od, og, gr, x, mk, mv, p):
    b, t, _ = x.shape
    n = b * t
    flat = lambda a: a.reshape(n, a.shape[-1])
    h1, qc = _mix_out(flat(od), flat(og), flat(gr), flat(x), p)
    h2, xt, pq = _cross(qc.reshape(b, t, D_MODEL), mk, mv, h1.reshape(b, t, D_MODEL), p)
    return _peer(flat(h2), flat(xt), flat(pq), p).reshape(b, t, D_MODEL)


def kernel(x_prompt, x_sample, mem_prompt, cache_diff_k, cache_diff_v, page_table, state_gla,
           cache_mem_k, cache_mem_v, attn_norm, w_in, diff_q_norm, diff_k_norm, lambda_q1,
           lambda_k1, lambda_q2, lambda_k2, diff_out_norm, gla_w_gate, gla_b_gate, gla_out_norm,
           w_o, cross_norm, mem_norm, cross_w_q, cross_w_kv, cross_q_norm, cross_k_norm,
           cross_w_o, ffn_norm, peer_w_q, peer_q_norm, peer_sub_keys, peer_u, peer_v):
    w = dict(attn_norm=attn_norm, w_in=w_in, diff_q_norm=diff_q_norm, diff_k_norm=diff_k_norm,
             lambda_q1=lambda_q1, lambda_k1=lambda_k1, lambda_q2=lambda_q2, lambda_k2=lambda_k2,
             diff_out_norm=diff_out_norm, gla_w_gate=gla_w_gate, gla_b_gate=gla_b_gate,
             gla_out_norm=gla_out_norm, w_o=w_o, cross_norm=cross_norm, mem_norm=mem_norm,
             cross_w_q=cross_w_q, cross_w_kv=cross_w_kv, cross_q_norm=cross_q_norm,
             cross_k_norm=cross_k_norm, cross_w_o=cross_w_o, ffn_norm=ffn_norm,
             peer_w_q=peer_w_q, peer_q_norm=peer_q_norm, peer_sub_keys=peer_sub_keys,
             peer_u=peer_u, peer_v=peer_v)
    b, s, _ = x_prompt.shape
    db, t, _ = x_sample.shape
    depth = w_in.shape[0]
    hp, hs = x_prompt, x_sample
    outs = [[] for _ in range(8)]
    for l in range(depth):
        p = _prep_layer(l, w)
        lam = _diff_lambda(w, l)

        dq, dk, dv, dk16, dv16, gq, gk, gv, la, gr = _proj_in(hp.reshape(b * s, D_MODEL), p)
        seq = lambda a: a.reshape(b, s, a.shape[-1])
        od = _diff_prompt(lam, seq(dq), seq(dk16), seq(dv16), p)
        st0 = jnp.zeros((b, GLA_HEADS // 2, LANES, LANES), F32)
        og, st = _gla(seq(gq), seq(gk), seq(gv), seq(la), st0, p)
        mk, mv = _mem_kv(mem_prompt.reshape(b * N_MEM, D_MODEL), p)
        mk = mk.reshape(b, N_MEM, D_MODEL)
        mv = mv.reshape(b, N_MEM, D_MODEL)
        hp = _tail(od, og, seq(gr), hp, mk, mv, p)
        outs[0].append(dk.reshape(b, s, DIFF_HEADS, DIFF_V))
        outs[1].append(dv.reshape(b, s, DIFF_HEADS, DIFF_V))
        outs[2].append(_pairs_to_state(st))
        outs[3].append(mk.reshape(b, N_MEM, CROSS_HEADS, CROSS_HD))
        outs[4].append(mv.reshape(b, N_MEM, CROSS_HEADS, CROSS_HD))

        dq, dk, dv, dk16, dv16, gq, gk, gv, la, gr = _proj_in(hs.reshape(db * t, D_MODEL), p)
        seq = lambda a: a.reshape(db, t, a.shape[-1])
        od = _diff_sample(lam, page_table, seq(dq), seq(dk16), seq(dv16), cache_diff_k[l],
                          cache_diff_v[l], p)
        tp = -(-t // GLA_SUB) * GLA_SUB
        padt = lambda a: jnp.pad(seq(a), ((0, 0), (0, tp - t), (0, 0)))
        og, st = _gla(padt(gq), padt(gk), padt(gv), padt(la), _state_to_pairs(state_gla[l]), p)
        hs_pad = jnp.pad(hs, ((0, 0), (0, tp - t), (0, 0)))
        y = _tail(jnp.pad(od, ((0, 0), (0, tp - t), (0, 0))).astype(BF16), og, padt(gr), hs_pad,
                  cache_mem_k[l].reshape(db, N_MEM, D_MODEL),
                  cache_mem_v[l].reshape(db, N_MEM, D_MODEL), p)
        hs = y[:, :t]
        outs[5].append(dk.reshape(db, t, DIFF_HEADS, DIFF_V))
        outs[6].append(dv.reshape(db, t, DIFF_HEADS, DIFF_V))
        outs[7].append(_pairs_to_state(st))

    stack = lambda xs: jnp.stack(xs)
    return (hp, hs, stack(outs[0]), stack(outs[1]), stack(outs[2]), stack(outs[3]), stack(outs[4]),
            stack(outs[5]), stack(outs[6]), stack(outs[7]))
```

```python
import functools
import math

import jax
import jax.numpy as jnp
from jax import lax
from jax.experimental import pallas as pl
from jax.experimental.pallas import tpu as pltpu

F32 = jnp.float32
BF16 = jnp.bfloat16
I32 = jnp.int32

EPS = 1e-6
D_MODEL = 1024
DIFF_HEADS = 8
DIFF_QK = 32
DIFF_V = 64
DIFF_W = DIFF_HEADS * DIFF_V
GLA_HEADS = 4
GLA_DK = 64
GLA_DV = 128
GLA_QW = GLA_HEADS * GLA_DK
GLA_VW = GLA_HEADS * GLA_DV
GLA_GATE_RANK = 16
GLA_TAU = 16.0
GLA_SUB = 16
CROSS_HEADS = 4
CROSS_HD = 256
N_MEM = 256
PEER_HEADS = 8
PEER_QD = 256
N_KEYS = 128
PEER_TOPK = 16
PEER_SEL = PEER_HEADS * PEER_TOPK
PAGE_SIZE = 128
LANES = 128
VMEM_LIMIT = 56 * 1024 * 1024

_OFF_GA = 2560
_OFF_GR = 2576


def _params(sem, vmem=None):
    return pltpu.CompilerParams(dimension_semantics=sem, vmem_limit_bytes=vmem)


def _rms(x, g):
    return x * lax.rsqrt(jnp.mean(x * x, axis=-1, keepdims=True) + EPS) * g


def _split_dot(x, w):
    hi = x.astype(BF16)
    lo = (x - hi.astype(F32)).astype(BF16)
    return (jnp.dot(hi, w, preferred_element_type=F32)
            + jnp.dot(lo, w, preferred_element_type=F32))


def _group_rms(x, bd, g):
    ms = _split_dot(x * x, bd)
    return x * lax.rsqrt(ms + EPS) * g


def _block_diag(width, group, value, dtype=BF16):
    i = jnp.arange(width) // group
    return jnp.where(i[:, None] == i[None, :], value, 0.0).astype(dtype)


def _proj_in_kernel(x_ref, g_ref, wa_ref, wgr_ref, wga_ref, qn_ref, kn_ref, bd_ref, wg_ref, bg_ref,
                    dq_ref, dk_ref, dv_ref, dk16_ref, dv16_ref, gq_ref, gk_ref, gv_ref, la_ref,
                    gr_ref):
    xb = _rms(x_ref[...], g_ref[...]).astype(BF16)
    ya = jnp.dot(xb, wa_ref[...], preferred_element_type=F32)
    bd = bd_ref[...]
    dq = _group_rms(ya[:, 0:512], bd, qn_ref[...])
    dk = _group_rms(ya[:, 512:1024], bd, kn_ref[...])
    dv = ya[:, 1024:1536]
    dq_ref[...] = dq.astype(BF16)
    dk_ref[...] = dk
    dv_ref[...] = dv
    dk16_ref[...] = dk.astype(BF16)
    dv16_ref[...] = dv.astype(BF16)
    gq_ref[...] = ya[:, 1536:1792] * (GLA_DK ** -0.5)
    gk_ref[...] = ya[:, 1792:2048]
    gv_ref[...] = ya[:, 2048:2560]
    gr_ref[...] = jnp.dot(xb, wgr_ref[...], preferred_element_type=F32)
    ga = jnp.dot(xb, wga_ref[...], preferred_element_type=F32)
    z = jnp.dot(ga.astype(BF16), wg_ref[...], preferred_element_type=F32) + bg_ref[...]
    log_sig = jnp.minimum(z, 0.0) - jnp.log(1.0 + jnp.exp(-jnp.abs(z)))
    la_ref[...] = log_sig * (1.0 / GLA_TAU)


def _proj_in(x, p, tm=256):
    n = x.shape[0]
    tm = min(tm, n)
    row = lambda w: pl.BlockSpec((tm, w), lambda i: (i, 0))
    full = lambda a: pl.BlockSpec(a.shape, lambda i: (0,) * a.ndim)
    consts = (p["attn_norm"], p["w_a"], p["w_gr"], p["w_ga"], p["dq_norm"], p["dk_norm"],
              p["bd32"], p["w_gate"], p["b_gate"])
    out_w = ((512, BF16), (512, F32), (512, F32), (512, BF16), (512, BF16), (256, F32), (256, F32),
             (512, F32), (256, F32), (512, F32))
    return pl.pallas_call(
        _proj_in_kernel,
        grid=(n // tm,),
        in_specs=[row(D_MODEL)] + [full(a) for a in consts],
        out_specs=[row(w) for w, _ in out_w],
        out_shape=[jax.ShapeDtypeStruct((n, w), dt) for w, dt in out_w],
        compiler_params=_params(("parallel",), VMEM_LIMIT),
        name="proj_in",
    )(x, *consts)


def _mix_out_kernel(od_ref, og_ref, gr_ref, x_ref, gn_ref, wod_ref, wog_ref, cn_ref, wq_ref,
                    qn_ref, h_ref, qc_ref):
    og = og_ref[...]
    gr = gr_ref[...]
    gn = gn_ref[...]
    parts = []
    for h in range(GLA_HEADS):
        sl = slice(h * GLA_DV, (h + 1) * GLA_DV)
        parts.append(_rms(og[:, sl], gn[:, sl]))
    ogn = jnp.concatenate(parts, axis=-1) * (gr * jax.nn.sigmoid(gr))
    mix = (jnp.dot(od_ref[...], wod_ref[...], preferred_element_type=F32)
           + jnp.dot(ogn.astype(BF16), wog_ref[...], preferred_element_type=F32))
    h1 = x_ref[...] + mix
    h_ref[...] = h1
    hn = _rms(h1, cn_ref[...]).astype(BF16)
    q = jnp.dot(hn, wq_ref[...], preferred_element_type=F32)
    qn = qn_ref[...]
    parts = []
    for h in range(CROSS_HEADS):
        sl = slice(h * CROSS_HD, (h + 1) * CROSS_HD)
        parts.append(_rms(q[:, sl], qn[:, sl]))
    qc_ref[...] = jnp.concatenate(parts, axis=-1).astype(BF16)


def _mix_out(od, og, gr, x, p, tm=256):
    n = x.shape[0]
    tm = min(tm, n)
    row = lambda w: pl.BlockSpec((tm, w), lambda i: (i, 0))
    full = lambda a: pl.BlockSpec(a.shape, lambda i: (0,) * a.ndim)
    consts = (p["gla_out_norm"], p["w_o_d"], p["w_o_g"], p["cross_norm"], p["cross_w_q"],
              p["cross_q_norm"])
    return pl.pallas_call(
        _mix_out_kernel,
        grid=(n // tm,),
        in_specs=[row(512), row(512), row(512), row(D_MODEL)] + [full(a) for a in consts],
        out_specs=[row(D_MODEL), row(D_MODEL)],
        out_shape=[jax.ShapeDtypeStruct((n, D_MODEL), F32), jax.ShapeDtypeStruct((n, D_MODEL), BF16)],
        compiler_params=_params(("parallel",), VMEM_LIMIT),
        name="mix_out",
    )(od, og, gr, x, *consts)


def _mem_kv_kernel(m_ref, mn_ref, wkv_ref, kn_ref, k_ref, v_ref):
    mb = _rms(m_ref[...], mn_ref[...]).astype(BF16)
    kv = jnp.dot(mb, wkv_ref[...], preferred_element_type=F32)
    kn = kn_ref[...]
    parts = []
    for h in range(CROSS_HEADS):
        sl = slice(h * CROSS_HD, (h + 1) * CROSS_HD)
        parts.append(_rms(kv[:, sl], kn[:, sl]))
    k_ref[...] = jnp.concatenate(parts, axis=-1)
    v_ref[...] = kv[:, D_MODEL:]


def _mem_kv(mem, p, tm=256):
    n = mem.shape[0]
    tm = min(tm, n)
    row = lambda w: pl.BlockSpec((tm, w), lambda i: (i, 0))
    full = lambda a: pl.BlockSpec(a.shape, lambda i: (0,) * a.ndim)
    consts = (p["mem_norm"], p["cross_w_kv"], p["cross_k_norm"])
    return pl.pallas_call(
        _mem_kv_kernel,
        grid=(n // tm,),
        in_specs=[row(D_MODEL)] + [full(a) for a in consts],
        out_specs=[row(D_MODEL), row(D_MODEL)],
        out_shape=[jax.ShapeDtypeStruct((n, D_MODEL), F32)] * 2,
        compiler_params=_params(("parallel",), VMEM_LIMIT),
        name="mem_kv",
    )(mem, *consts)


def _cross_kernel(qc_ref, mk_ref, mv_ref, h1_ref, wo_ref, fn_ref, wpq_ref, pqn_ref,
                  h2_ref, xt_ref, pq_ref):
    qc = qc_ref[0]
    mk = mk_ref[0].astype(BF16)
    mv = mv_ref[0].astype(BF16)
    parts = []
    for h in range(CROSS_HEADS):
        sl = slice(h * CROSS_HD, (h + 1) * CROSS_HD)
        s = lax.dot_general(qc[:, sl], mk[:, sl], (((1,), (1,)), ((), ())),
                            preferred_element_type=F32)
        e = jnp.exp(s - jnp.max(s, axis=-1, keepdims=True))
        prob = e / jnp.sum(e, axis=-1, keepdims=True)
        parts.append(jnp.dot(prob.astype(BF16), mv[:, sl], preferred_element_type=F32))
    o = jnp.concatenate(parts, axis=-1).astype(BF16)
    h2 = h1_ref[0] + jnp.dot(o, wo_ref[...], preferred_element_type=F32)
    h2_ref[0] = h2
    xt = _rms(h2, fn_ref[...])
    xt_ref[0] = xt
    pq = jnp.dot(xt.astype(BF16), wpq_ref[...], preferred_element_type=F32)
    pqn = pqn_ref[...]
    parts = []
    for h in range(PEER_HEADS):
        sl = slice(h * PEER_QD, (h + 1) * PEER_QD)
        parts.append(_rms(pq[:, sl], pqn[:, sl]))
    pq_ref[0] = jnp.concatenate(parts, axis=-1).astype(BF16)


def _cross(qc, mk, mv, h1, p, tq=256):
    b, t, _ = qc.shape
    tq = min(tq, t)
    row = lambda w: pl.BlockSpec((1, tq, w), lambda i, j: (i, j, 0))
    mem = pl.BlockSpec((1, N_MEM, D_MODEL), lambda i, j: (i, 0, 0))
    full = lambda a: pl.BlockSpec(a.shape, lambda i, j: (0,) * a.ndim)
    consts = (p["cross_w_o"], p["ffn_norm"], p["peer_w_q"], p["peer_q_norm"])
    return pl.pallas_call(
        _cross_kernel,
        grid=(b, t // tq),
        in_specs=[row(D_MODEL), mem, mem, row(D_MODEL)] + [full(a) for a in consts],
        out_specs=[row(D_MODEL), row(D_MODEL), row(2048)],
        out_shape=[jax.ShapeDtypeStruct((b, t, D_MODEL), F32),
                   jax.ShapeDtypeStruct((b, t, D_MODEL), F32),
                   jax.ShapeDtypeStruct((b, t, 2048), BF16)],
        compiler_params=_params(("parallel", "parallel"), VMEM_LIMIT),
        name="cross_attn",
    )(qc, mk, mv, h1, *consts)


def _diff_prompt_kernel(lam_ref, qt_ref, k_ref, vt_ref, on_ref, o_ref, m_scr, l_scr, acc_scr, *, tq):
    qi = pl.program_id(1)
    lam = lam_ref[0, 0]
    cols = 4 * tq
    chan = lax.broadcasted_iota(I32, (LANES, cols), 0)
    my_map = chan // DIFF_QK == lax.broadcasted_iota(I32, (LANES, cols), 1) // tq
    causal = (lax.broadcasted_iota(I32, (tq, cols), 0)
              <= lax.broadcasted_iota(I32, (tq, cols), 1) % tq)
    top_head = lax.broadcasted_iota(I32, (LANES, tq), 0) < DIFF_V
    for p in range(DIFF_HEADS // 2):
        sl = slice(p * LANES, (p + 1) * LANES)
        qtp = qt_ref[0, sl, :]
        qs = jnp.where(my_map, jnp.concatenate([qtp] * 4, axis=1), jnp.zeros((LANES, cols), BF16))
        m_scr[...] = jnp.full((1, cols), -jnp.inf, F32)
        l_scr[...] = jnp.zeros((1, cols), F32)
        acc_scr[...] = jnp.zeros((LANES, cols), F32)

        def block(kb, masked, qs=qs):
            t0 = pl.multiple_of(kb * tq, tq)
            s = jnp.dot(k_ref[0, pl.ds(t0, tq), sl], qs, preferred_element_type=F32)
            if masked:
                s = jnp.where(causal, s, -jnp.inf)
            m_old = m_scr[...]
            m_new = jnp.maximum(m_old, jnp.max(s, axis=0, keepdims=True))
            alpha = jnp.exp(m_old - m_new)
            e = jnp.exp(s - m_new)
            m_scr[...] = m_new
            l_scr[...] = alpha * l_scr[...] + jnp.sum(e, axis=0, keepdims=True)
            acc_scr[...] = alpha * acc_scr[...] + jnp.dot(
                vt_ref[0, sl, pl.ds(t0, tq)], e.astype(BF16), preferred_element_type=F32)

        def body(kb, carry):
            block(kb, False)
            return carry

        lax.fori_loop(0, qi, body, 0)
        block(qi, True)
        on = acc_scr[...] / l_scr[...]
        ot = jnp.where(top_head, on[:, :tq] - lam * on[:, tq:2 * tq],
                       on[:, 2 * tq:3 * tq] - lam * on[:, 3 * tq:])
        sq = ot * ot
        ms = jnp.where(top_head, jnp.sum(sq[:DIFF_V], axis=0, keepdims=True),
                       jnp.sum(sq[DIFF_V:], axis=0, keepdims=True)) * (1.0 / DIFF_V)
        ot = ot * lax.rsqrt(ms + EPS) * on_ref[sl, :]
        o_ref[0, :, sl] = ot.T.astype(BF16)


def _diff_prompt(lam, dq, dk16, dv16, p, tq=512):
    b, s, _ = dq.shape
    tq = min(tq, s)
    qt = jnp.swapaxes(dq, 1, 2)
    vt = jnp.swapaxes(dv16, 1, 2)
    return pl.pallas_call(
        functools.partial(_diff_prompt_kernel, tq=tq),
        grid=(b, s // tq),
        in_specs=[pl.BlockSpec(memory_space=pltpu.SMEM),
                  pl.BlockSpec((1, DIFF_W, tq), lambda i, j: (i, 0, j)),
                  pl.BlockSpec((1, s, DIFF_W), lambda i, j: (i, 0, 0)),
                  pl.BlockSpec((1, DIFF_W, s), lambda i, j: (i, 0, 0)),
                  pl.BlockSpec((DIFF_W, 1), lambda i, j: (0, 0))],
        out_specs=pl.BlockSpec((1, tq, DIFF_W), lambda i, j: (i, j, 0)),
        out_shape=jax.ShapeDtypeStruct((b, s, DIFF_W), BF16),
        scratch_shapes=[pltpu.VMEM((1, 4 * tq), F32), pltpu.VMEM((1, 4 * tq), F32),
                        pltpu.VMEM((LANES, 4 * tq), F32)],
        compiler_params=_params(("parallel", "parallel"), VMEM_LIMIT),
        name="diff_prompt",
    )(lam, qt, dk16, vt, p["diff_out_norm"].reshape(DIFF_W, 1))


def _gla_kernel(q_ref, k_ref, v_ref, la_ref, st0_ref, tri_ref, ones_ref, o_ref, stout_ref,
                st_scr, b_scr, *, ts, nb):
    si = pl.program_id(1)

    @pl.when(si == 0)
    def _():
        st_scr[...] = st0_ref[...]

    tri = tri_ref[...]
    ones = ones_ref[...]
    sub_q = lax.broadcasted_iota(I32, (ts, GLA_QW), 0) % GLA_SUB
    sub_v = lax.broadcasted_iota(I32, (ts, GLA_VW), 0) % GLA_SUB
    for s in range(nb):
        la = la_ref[s]
        hi = la.astype(BF16)
        r1 = la - hi.astype(F32)
        mid = r1.astype(BF16)
        lo = (r1 - mid.astype(F32)).astype(BF16)
        b = (jnp.dot(tri, hi, preferred_element_type=F32)
             + jnp.dot(tri, mid, preferred_element_type=F32)
             + jnp.dot(tri, lo, preferred_element_type=F32))
        b_scr[s] = b

        q = q_ref[s]
        k = k_ref[s]
        v = v_ref[s]
        o_intra = jnp.zeros((ts, GLA_VW), F32)
        for d in range(GLA_SUB):
            if d == 0:
                kd, bd, vd = k, b, v
            else:
                kd = pltpu.roll(k, d, 0)
                bd = pltpu.roll(b, d, 0)
                vd = pltpu.roll(v, d, 0)
            ok = sub_q >= d
            e = jnp.where(ok, q * kd * jnp.exp(jnp.where(ok, b - bd, 0.0)), 0.0)
            w = jnp.dot(e.astype(BF16), ones, preferred_element_type=F32)
            o_intra = o_intra + w * jnp.where(sub_v >= d, vd, 0.0)
        o_ref[s] = o_intra

    lane = lax.broadcasted_iota(I32, (GLA_SUB, LANES), 1)
    lane_sq = lax.broadcasted_iota(I32, (LANES, LANES), 1)

    def step(c, carry):
        t0 = pl.multiple_of(c * GLA_SUB, GLA_SUB)
        for s in range(nb):
            bc = b_scr[s, pl.ds(t0, GLA_SUB), :]
            qc = q_ref[s, pl.ds(t0, GLA_SUB), :]
            kc = k_ref[s, pl.ds(t0, GLA_SUB), :]
            vc = v_ref[s, pl.ds(t0, GLA_SUB), :]
            bl = bc[GLA_SUB - 1:GLA_SUB, :]
            qe = qc * jnp.exp(bc)
            ke = (kc * jnp.exp(bl - bc)).astype(BF16)
            dec = jnp.exp(bl)
            o_parts = []
            for p in range(GLA_HEADS // 2):
                sl = slice(p * LANES, (p + 1) * LANES)
                st = st_scr[s, p]
                stb = st.astype(BF16)
                qp = qe[:, sl]
                upd = []
                for hh in range(2):
                    mine = (lane < GLA_DK) if hh == 0 else (lane >= GLA_DK)
                    qm = jnp.where(mine, qp, 0.0).astype(BF16)
                    o_parts.append(lax.dot_general(qm, stb, (((1,), (1,)), ((), ())),
                                                   preferred_element_type=F32))
                    vh = vc[:, (2 * p + hh) * GLA_DV:(2 * p + hh + 1) * GLA_DV]
                    upd.append(jnp.dot(vh.T.astype(BF16), ke[:, sl], preferred_element_type=F32))
                st_scr[s, p] = dec[:, sl] * st + jnp.where(lane_sq < GLA_DK, upd[0], upd[1])
            o_ref[s, pl.ds(t0, GLA_SUB), :] += jnp.concatenate(o_parts, axis=-1)
        return carry

    lax.fori_loop(0, ts // GLA_SUB, step, 0)

    @pl.when(si == pl.num_programs(1) - 1)
    def _():
        stout_ref[...] = st_scr[...]


def _gla(gq, gk, gv, la, st0, p, ts=256):
    b, t, _ = gq.shape
    ts = min(ts, t)
    nb = 2 if b % 2 == 0 else 1
    i = jnp.arange(ts)
    tri = ((i[:, None] // GLA_SUB == i[None, :] // GLA_SUB) & (i[None, :] <= i[:, None])).astype(BF16)
    ones = _block_diag_rect(GLA_QW, GLA_DK, GLA_VW, GLA_DV)
    seq = lambda w: pl.BlockSpec((nb, ts, w), lambda bi, si: (bi, si, 0))
    stspec = pl.BlockSpec((nb, GLA_HEADS // 2, LANES, LANES), lambda bi, si: (bi, 0, 0, 0))
    return pl.pallas_call(
        functools.partial(_gla_kernel, ts=ts, nb=nb),
        grid=(b // nb, t // ts),
        in_specs=[seq(GLA_QW), seq(GLA_QW), seq(GLA_VW), seq(GLA_QW), stspec,
                  pl.BlockSpec((ts, ts), lambda bi, si: (0, 0)),
                  pl.BlockSpec((GLA_QW, GLA_VW), lambda bi, si: (0, 0))],
        out_specs=[seq(GLA_VW), stspec],
        out_shape=[jax.ShapeDtypeStruct((b, t, GLA_VW), F32),
                   jax.ShapeDtypeStruct((b, GLA_HEADS // 2, LANES, LANES), F32)],
        scratch_shapes=[pltpu.VMEM((nb, GLA_HEADS // 2, LANES, LANES), F32),
                        pltpu.VMEM((nb, ts, GLA_QW), F32)],
        compiler_params=_params(("parallel", "arbitrary"), VMEM_LIMIT),
        name="gla",
    )(gq, gk, gv, la, st0, tri, ones)


PAGES_PER_STEP = 8
NEW_T = 8


def _diff_sample_kernel(pt_ref, lam_ref, q_ref, *rest):
    del pt_ref
    np_ = PAGES_PER_STEP
    k_refs = rest[:np_]
    v_refs = rest[np_:2 * np_]
    kn_ref, vn_ref, on_ref, o_ref, m_scr, l_scr, acc_scr = rest[2 * np_:]
    i = pl.program_id(1)
    nt = (((1,), (1,)), ((), ()))

    @pl.when(i == 0)
    def _():
        m_scr[...] = jnp.full(m_scr.shape, -jnp.inf, F32)
        l_scr[...] = jnp.zeros(l_scr.shape, F32)
        acc_scr[...] = jnp.zeros(acc_scr.shape, F32)

    q = q_ref[0]

    def update(scores, value_products):
        m_old = m_scr[...]
        m_new = m_old
        for s in scores:
            m_new = jnp.maximum(m_new, jnp.max(s, axis=-1, keepdims=True))
        alpha = jnp.exp(m_old - m_new)
        l = alpha * l_scr[...]
        acc = alpha * acc_scr[...]
        for s, pv in zip(scores, value_products):
            e = jnp.exp(s - m_new)
            l = l + jnp.sum(e, axis=-1, keepdims=True)
            acc = acc + pv(e.astype(BF16))
        m_scr[...] = m_new
        l_scr[...] = l
        acc_scr[...] = acc

    def page_t(ref):
        return ref[...].reshape(DIFF_W, PAGE_SIZE).astype(BF16)

    scores = [jnp.dot(q, page_t(k_ref), preferred_element_type=F32) for k_ref in k_refs]
    update(scores, [lambda e, v_ref=v_ref: lax.dot_general(e, page_t(v_ref), nt,
                                                           preferred_element_type=F32)
                    for v_ref in v_refs])

    @pl.when(i == pl.num_programs(1) - 1)
    def _():
        s = lax.dot_general(q, kn_ref[0], nt, preferred_element_type=F32)
        t_q = lax.broadcasted_iota(I32, s.shape, 0) % NEW_T
        t_k = lax.broadcasted_iota(I32, s.shape, 1)
        update([jnp.where(t_k <= t_q, s, -jnp.inf)],
               [lambda e: jnp.dot(e, vn_ref[0], preferred_element_type=F32)])
        accn = acc_scr[...] / l_scr[...]
        rows = 2 * NEW_T
        lane = lax.broadcasted_iota(I32, (rows, DIFF_W), 1)
        o16 = jnp.zeros((rows, DIFF_W), F32)
        for h in range(DIFF_HEADS):
            o16 = jnp.where(lane // DIFF_V == h, accn[h * rows:(h + 1) * rows, :], o16)
        o = o16[:NEW_T] - lam_ref[0, 0] * o16[NEW_T:]
        lane = lax.broadcasted_iota(I32, (NEW_T, DIFF_W), 1)
        sq = o * o
        ms = jnp.zeros_like(o)
        for h in range(DIFF_HEADS):
            mine = lane // DIFF_V == h
            ms = jnp.where(mine, jnp.sum(jnp.where(mine, sq, 0.0), axis=-1, keepdims=True), ms)
        o_ref[0] = o * lax.rsqrt(ms * (1.0 / DIFF_V) + EPS) * on_ref[...]


def _diff_sample(lam, page_table, dq, dk16, dv16, cache_k, cache_v, p):
    db, t, _ = dq.shape
    n_pages = page_table.shape[1]
    np_ = PAGES_PER_STEP
    a = dq.reshape(db, t, 16, DIFF_QK).transpose(0, 2, 1, 3)
    q_bd = (a[:, :, :, None, :] * jnp.eye(16, dtype=BF16)[None, :, None, :, None]
            ).reshape(db, 16 * t, DIFF_W)
    pad = lambda x: jnp.pad(x, ((0, 0), (0, LANES - t), (0, 0)))
    ck = cache_k.transpose(0, 2, 3, 1)
    cv = cache_v.transpose(0, 2, 3, 1)

    def page_spec(n):
        return pl.BlockSpec((None, DIFF_HEADS, DIFF_V, PAGE_SIZE),
                            lambda b, i, pt: (pt[b, i * np_ + n], 0, 0, 0))

    per_b = lambda r: pl.BlockSpec((1, r, DIFF_W), lambda b, i, pt: (b, 0, 0))
    grid_spec = pltpu.PrefetchScalarGridSpec(
        num_scalar_prefetch=1,
        grid=(db, n_pages // np_),
        in_specs=[pl.BlockSpec(memory_space=pltpu.SMEM), per_b(16 * t)]
        + [page_spec(n) for n in range(np_)] + [page_spec(n) for n in range(np_)]
        + [per_b(LANES), per_b(LANES), pl.BlockSpec((1, DIFF_W), lambda b, i, pt: (0, 0))],
        out_specs=per_b(t),
        scratch_shapes=[pltpu.VMEM((16 * t, 1), F32), pltpu.VMEM((16 * t, 1), F32),
                        pltpu.VMEM((16 * t, DIFF_W), F32)],
    )
    return pl.pallas_call(
        _diff_sample_kernel,
        grid_spec=grid_spec,
        out_shape=jax.ShapeDtypeStruct((db, t, DIFF_W), F32),
        compiler_params=_params(("parallel", "arbitrary"), VMEM_LIMIT),
        name="diff_sample",
    )(page_table, lam, q_bd, *([ck] * np_), *([cv] * np_), pad(dk16), pad(dv16),
      p["diff_out_norm"])


def _top16_rows(s, row):
    vals, idxs = [], []
    sentinel = s.shape[0]
    for _ in range(PEER_TOPK):
        m = jnp.max(s, axis=0, keepdims=True)
        idx = jnp.min(jnp.where(s == m, row, sentinel), axis=0, keepdims=True)
        vals.append(m)
        idxs.append(idx)
        s = jnp.where(row == idx, -jnp.inf, s)
    return jnp.concatenate(vals, axis=0), jnp.concatenate(idxs, axis=0)


def _pick_row(table, row, idx):
    return jnp.max(jnp.where(row == idx, table, -1), axis=0, keepdims=True)


_CAND_WIDTH = [PEER_TOPK // (a + 1) for a in range(PEER_TOPK)]
_CAND_ROWS = -(-sum(_CAND_WIDTH) // 8) * 8


def _route_kernel(pq_ref, sk_ref, e_ref, g_ref, *, tb):
    nt = (((1,), (1,)), ((), ()))
    row_k = lax.broadcasted_iota(I32, (N_KEYS, tb), 0)
    row_c = lax.broadcasted_iota(I32, (_CAND_ROWS, tb), 0)
    n_pad = _CAND_ROWS - sum(_CAND_WIDTH)
    e_rows, g_rows = [], []
    for h in range(PEER_HEADS):
        tops = []
        for c in range(2):
            hc = 2 * h + c
            s = lax.dot_general(sk_ref[hc], pq_ref[:, hc * N_KEYS:(hc + 1) * N_KEYS], nt,
                                preferred_element_type=F32)
            tops.append(_top16_rows(s, row_k))
        (s1, i1), (s2, i2) = tops
        cand = jnp.concatenate(
            [jnp.broadcast_to(s1[a:a + 1], (nb, tb)) + s2[:nb] for a, nb in enumerate(_CAND_WIDTH)]
            + [jnp.full((n_pad, tb), -jnp.inf, F32)], axis=0)
        cand_e = jnp.concatenate(
            [jnp.broadcast_to(i1[a:a + 1], (nb, tb)) * N_KEYS + i2[:nb]
             for a, nb in enumerate(_CAND_WIDTH)] + [jnp.zeros((n_pad, tb), I32)], axis=0)
        top_s, j = _top16_rows(cand, row_c)
        e_rows.append(jnp.concatenate(
            [_pick_row(cand_e, row_c, j[r:r + 1]) for r in range(PEER_TOPK)], axis=0))
        ex = jnp.exp(top_s - top_s[0:1])
        g_rows.append(ex / jnp.sum(ex, axis=0, keepdims=True))
    e_ref[...] = (jnp.concatenate(e_rows, axis=0) * HALF_ROWS).T
    g_ref[...] = jnp.concatenate(g_rows, axis=0).T


def _route(pq, p, tb=128):
    n = pq.shape[0]
    tb = min(tb, n)
    return pl.pallas_call(
        functools.partial(_route_kernel, tb=tb),
        grid=(n // tb,),
        in_specs=[pl.BlockSpec((tb, 2048), lambda i: (i, 0)),
                  pl.BlockSpec((2 * PEER_HEADS, N_KEYS, N_KEYS), lambda i: (0, 0, 0))],
        out_specs=[pl.BlockSpec((tb, PEER_SEL), lambda i: (i, 0))] * 2,
        out_shape=[jax.ShapeDtypeStruct((n, PEER_SEL), I32), jax.ShapeDtypeStruct((n, PEER_SEL), F32)],
        compiler_params=_params(("parallel",), VMEM_LIMIT),
        name="peer_route",
    )(pq, p["peer_sub_keys"])


HALF_ROWS = 4


def _table_row(tab_ref, row_start):
    return tab_ref[pl.ds(pl.multiple_of(row_start, HALF_ROWS), HALF_ROWS), :]


def _unpack_pair(row):
    lo = lax.bitcast_convert_type(row << 16, F32)
    hi = lax.bitcast_convert_type(row & jnp.uint32(0xFFFF0000), F32)
    return lo, hi


def _gelu_tanh(x):
    return 0.5 * x * (1.0 + jnp.tanh(math.sqrt(2.0 / math.pi) * (x + 0.044715 * (x * x * x))))


def _gather_rows(e_ref, tab_ref, t, stage):
    for j in range(PEER_SEL):
        stage[j * HALF_ROWS:(j + 1) * HALF_ROWS, :] = _table_row(tab_ref, e_ref[t, j])


def _peer_dot_kernel(e_ref, xt_ref, g_ref, tab_ref, gsum_ref, w_ref, stage0, stage1,
                     part0, part1, *, tb):
    lane = lax.broadcasted_iota(I32, (PEER_SEL, tb), 1)
    gsum = gsum_ref[...]
    chunks = 2 * HALF_ROWS
    gsum16 = gsum_ref[:LANES // chunks, :LANES]
    pairs = PEER_SEL * chunks // 16

    def group_sums(t):
        prods = []
        for k, stage in enumerate((stage0, stage1)):
            x = xt_ref[jnp.maximum(t + k, 0)]
            rows = pltpu.bitcast(stage[...], BF16).reshape(pairs, 16, LANES)
            prods.append((rows * x[None]).reshape(PEER_SEL * chunks, LANES))
        per = LANES // chunks
        for c in range(PEER_SEL // per):
            for prod, part in zip(prods, (part0, part1)):
                part[c * per:(c + 1) * per, :] = jnp.dot(
                    gsum16, prod[c * LANES:(c + 1) * LANES, :], preferred_element_type=F32)

    def lane_sums(t, part, ht):
        return jnp.where(lane == t, jnp.sum(part[...], axis=1, keepdims=True), ht)

    for ref in (stage0, stage1, part0, part1):
        ref[...] = jnp.zeros(ref.shape, ref.dtype)

    def token_pair(i, ht):
        t0 = 2 * i
        ht = lane_sums(t0 - 4, part0, ht)
        ht = lane_sums(t0 - 3, part1, ht)
        group_sums(t0 - 2)
        _gather_rows(e_ref, tab_ref, t0, stage0)
        _gather_rows(e_ref, tab_ref, t0 + 1, stage1)
        return ht

    ht = lax.fori_loop(0, tb // 2, token_pair, jnp.zeros((PEER_SEL, tb), F32))
    ht = lane_sums(tb - 4, part0, ht)
    ht = lane_sums(tb - 3, part1, ht)
    group_sums(tb - 2)
    ht = lane_sums(tb - 2, part0, ht)
    ht = lane_sums(tb - 1, part1, ht)
    w = g_ref[...] * _gelu_tanh(ht.T)
    w_ref[...] = jnp.dot(w.astype(BF16), gsum, preferred_element_type=F32)


def _peer_dot(e, xt3, g, tab, tb=128):
    n = e.shape[0]
    tb = min(tb, n)
    gsum = _block_diag_rect(PEER_SEL, 1, PEER_SEL * 2 * HALF_ROWS, 2 * HALF_ROWS)
    return pl.pallas_call(
        functools.partial(_peer_dot_kernel, tb=tb),
        grid=(n // tb,),
        in_specs=[pl.BlockSpec((tb, PEER_SEL), lambda i: (i, 0), memory_space=pltpu.SMEM),
                  pl.BlockSpec((tb, 16, LANES), lambda i: (i, 0, 0)),
                  pl.BlockSpec((tb, PEER_SEL), lambda i: (i, 0)),
                  pl.BlockSpec(memory_space=pltpu.VMEM),
                  pl.BlockSpec(gsum.shape, lambda i: (0, 0))],
        out_specs=pl.BlockSpec((tb, D_MODEL), lambda i: (i, 0)),
        out_shape=jax.ShapeDtypeStruct((n, D_MODEL), F32),
        scratch_shapes=[pltpu.VMEM((PEER_SEL * HALF_ROWS, LANES), jnp.uint32)] * 2
        + [pltpu.VMEM((PEER_SEL, LANES), F32)] * 2,
        compiler_params=_params(("arbitrary",), VMEM_LIMIT),
        name="peer_dot",
    )(e, xt3, g, tab, gsum)


def _peer_sum_kernel(e_ref, w_ref, h_ref, tab_ref, y_ref, stage0, stage1, *, tb):
    out_rows = 2 * HALF_ROWS
    q = lax.broadcasted_iota(I32, (2 * out_rows, D_MODEL), 0)
    lane = lax.broadcasted_iota(I32, (2 * out_rows, D_MODEL), 1)
    mine = (lane % out_rows == 2 * (q % HALF_ROWS) + q // HALF_ROWS) & (q < out_rows)

    def gather(t, stage):
        _gather_rows(e_ref, tab_ref, t, stage)

    def weighted_sum(t, stage):
        wrow = w_ref[pl.ds(t, 1), :]
        lhs = jnp.where(mine, wrow, 0.0).astype(BF16)
        rows = pltpu.bitcast(stage[...], BF16)
        out = jnp.dot(lhs, rows, preferred_element_type=F32)
        y_ref[t] = h_ref[t] + out[:out_rows]

    stage0[...] = jnp.zeros(stage0.shape, stage0.dtype)
    stage1[...] = jnp.zeros(stage1.shape, stage1.dtype)

    def token_pair(i, carry):
        t0 = 2 * i
        weighted_sum(jnp.maximum(t0 - 2, 0), stage0)
        weighted_sum(jnp.maximum(t0 - 1, 0), stage1)
        gather(t0, stage0)
        gather(t0 + 1, stage1)
        return carry

    lax.fori_loop(0, tb // 2, token_pair, 0)
    weighted_sum(tb - 2, stage0)
    weighted_sum(tb - 1, stage1)


def _peer_sum(e, w, h3, tab, tb=128):
    n = e.shape[0]
    tb = min(tb, n)
    tok = lambda: pl.BlockSpec((tb, 8, LANES), lambda i: (i, 0, 0))
    stage = pltpu.VMEM((PEER_SEL * HALF_ROWS, LANES), jnp.uint32)
    return pl.pallas_call(
        functools.partial(_peer_sum_kernel, tb=tb),
        grid=(n // tb,),
        in_specs=[pl.BlockSpec((tb, PEER_SEL), lambda i: (i, 0), memory_space=pltpu.SMEM),
                  pl.BlockSpec((tb, D_MODEL), lambda i: (i, 0)), tok(),
                  pl.BlockSpec(memory_space=pltpu.VMEM)],
        out_specs=tok(),
        out_shape=jax.ShapeDtypeStruct((n, 8, LANES), F32),
        scratch_shapes=[stage, stage],
        compiler_params=_params(("arbitrary",), VMEM_LIMIT),
        name="peer_sum",
    )(e, w, h3, tab)


def _pack_table(w):
    bits = lax.bitcast_convert_type(w.astype(BF16), jnp.uint16).astype(jnp.uint32)
    half = D_MODEL // 2
    return (bits[:, :half] | (bits[:, half:] << 16)).reshape(w.shape[0] * HALF_ROWS, LANES)


def _block_diag_rect(rows, rgroup, cols, cgroup):
    r = jnp.arange(rows) // rgroup
    c = jnp.arange(cols) // cgroup
    return (r[:, None] == c[None, :]).astype(BF16)


def _state_to_pairs(s):
    b = s.shape[0]
    return s.reshape(b, 2, 2, GLA_DK, GLA_DV).transpose(0, 1, 4, 2, 3).reshape(b, 2, LANES, LANES)


def _pairs_to_state(st):
    b = st.shape[0]
    return st.reshape(b, 2, GLA_DV, 2, GLA_DK).transpose(0, 1, 3, 4, 2).reshape(
        b, GLA_HEADS, GLA_DK, GLA_DV)


def _lambda_init(layer):
    return 0.8 - 0.6 * math.exp(-0.3 * layer)


def _prep_layer(l, w):
    row = lambda a: a.reshape(1, -1).astype(F32)
    w_in = w["w_in"][l]
    lam_init = _lambda_init(l)
    return {
        "attn_norm": row(w["attn_norm"][l]),
        "w_a": w_in[:, :_OFF_GA].astype(BF16),
        "w_ga": jnp.pad(w_in[:, _OFF_GA:_OFF_GR], ((0, 0), (0, LANES - GLA_GATE_RANK))).astype(BF16),
        "w_gr": w_in[:, _OFF_GR:].astype(BF16),
        "dq_norm": row(jnp.tile(w["diff_q_norm"][l], 2 * DIFF_HEADS)) * (DIFF_QK ** -0.5),
        "dk_norm": row(jnp.tile(w["diff_k_norm"][l], 2 * DIFF_HEADS)),
        "bd32": _block_diag(DIFF_W, DIFF_QK, 1.0 / DIFF_QK),
        "w_gate": jnp.pad(w["gla_w_gate"][l], ((0, LANES - GLA_GATE_RANK), (0, 0))).astype(BF16),
        "b_gate": row(w["gla_b_gate"][l]),
        "diff_out_norm": row(jnp.tile(w["diff_out_norm"][l], DIFF_HEADS)) * (1.0 - lam_init),
        "gla_out_norm": row(jnp.tile(w["gla_out_norm"][l], GLA_HEADS)),
        "w_o_d": w["w_o"][l][:DIFF_W].astype(BF16),
        "w_o_g": w["w_o"][l][DIFF_W:].astype(BF16),
        "cross_norm": row(w["cross_norm"][l]),
        "cross_w_q": w["cross_w_q"][l].astype(BF16),
        "cross_q_norm": row(jnp.tile(w["cross_q_norm"][l], CROSS_HEADS)) * (CROSS_HD ** -0.5),
        "mem_norm": row(w["mem_norm"][l]),
        "cross_w_kv": w["cross_w_kv"][l].astype(BF16),
        "cross_k_norm": row(jnp.tile(w["cross_k_norm"][l], CROSS_HEADS)),
        "cross_w_o": w["cross_w_o"][l].astype(BF16),
        "ffn_norm": row(w["ffn_norm"][l]),
        "peer_w_q": w["peer_w_q"][l].astype(BF16),
        "peer_q_norm": row(jnp.tile(w["peer_q_norm"][l], PEER_HEADS)),
        "peer_sub_keys": w["peer_sub_keys"][l].reshape(2 * PEER_HEADS, N_KEYS, N_KEYS).astype(BF16),
        "peer_u": _pack_table(w["peer_u"][l]),
        "peer_v": _pack_table(w["peer_v"][l]),
    }


def _diff_lambda(w, l):
    f = lambda a: a[l].astype(F32)
    lam = (jnp.exp(jnp.sum(f(w["lambda_q1"]) * f(w["lambda_k1"])))
           - jnp.exp(jnp.sum(f(w["lambda_q2"]) * f(w["lambda_k2"]))) + _lambda_init(l))
    return lam.reshape(1, 1)


def _peer(h2, xt, pq, p):
    n = h2.shape[0]
    e, g = _route(pq, p)
    x8 = xt.astype(BF16).reshape(n, 2, HALF_ROWS, LANES).transpose(0, 2, 1, 3).reshape(n, 8, LANES)
    w = _peer_dot(e, jnp.concatenate([x8, x8], axis=1), g, p["peer_u"])
    return _peer_sum(e, w, h2.reshape(n, 8, LANES), p["peer_v"]).reshape(n, D_MODEL)


def _tail(od, og, gr, x, mk, mv, p):
    b, t, _ = x.shape
    n = b * t
    flat = lambda a: a.reshape(n, a.shape[-1])
    h1, qc = _mix_out(flat(od), flat(og), flat(gr), flat(x), p)
    h2, xt, pq = _cross(qc.reshape(b, t, D_MODEL), mk, mv, h1.reshape(b, t, D_MODEL), p)
    return _peer(flat(h2), flat(xt), flat(pq), p).reshape(b, t, D_MODEL)


def kernel(x_prompt, x_sample, mem_prompt, cache_diff_k, cache_diff_v, page_table, state_gla,
           cache_mem_k, cache_mem_v, attn_norm, w_in, diff_q_norm, diff_k_norm, lambda_q1,
           lambda_k1, lambda_q2, lambda_k2, diff_out_norm, gla_w_gate, gla_b_gate, gla_out_norm,
           w_o, cross_norm, mem_norm, cross_w_q, cross_w_kv, cross_q_norm, cross_k_norm,
           cross_w_o, ffn_norm, peer_w_q, peer_q_norm, peer_sub_keys, peer_u, peer_v):
    w = dict(attn_norm=attn_norm, w_in=w_in, diff_q_norm=diff_q_norm, diff_k_norm=diff_k_norm,
             lambda_q1=lambda_q1, lambda_k1=lambda_k1, lambda_q2=lambda_q2, lambda_k2=lambda_k2,
             diff_out_norm=diff_out_norm, gla_w_gate=gla_w_gate, gla_b_gate=gla_b_gate,
             gla_out_norm=gla_out_norm, w_o=w_o, cross_norm=cross_norm, mem_norm=mem_norm,
             cross_w_q=cross_w_q, cross_w_kv=cross_w_kv, cross_q_norm=cross_q_norm,
             cross_k_norm=cross_k_norm, cross_w_o=cross_w_o, ffn_norm=ffn_norm,
             peer_w_q=peer_w_q, peer_q_norm=peer_q_norm, peer_sub_keys=peer_sub_keys,
             peer_u=peer_u, peer_v=peer_v)
    b, s, _ = x_prompt.shape
    db, t, _ = x_sample.shape
    depth = w_in.shape[0]
    hp, hs = x_prompt, x_sample
    outs = [[] for _ in range(8)]
    for l in range(depth):
        p = _prep_layer(l, w)
        lam = _diff_lambda(w, l)

        dq, dk, dv, dk16, dv16, gq, gk, gv, la, gr = _proj_in(hp.reshape(b * s, D_MODEL), p)
        seq = lambda a: a.reshape(b, s, a.shape[-1])
        od = _diff_prompt(lam, seq(dq), seq(dk16), seq(dv16), p)
        st0 = jnp.zeros((b, GLA_HEADS // 2, LANES, LANES), F32)
        og, st = _gla(seq(gq), seq(gk), seq(gv), seq(la), st0, p)
        mk, mv = _mem_kv(mem_prompt.reshape(b * N_MEM, D_MODEL), p)
        mk = mk.reshape(b, N_MEM, D_MODEL)
        mv = mv.reshape(b, N_MEM, D_MODEL)
        hp = _tail(od, og, seq(gr), hp, mk, mv, p)
        outs[0].append(dk.reshape(b, s, DIFF_HEADS, DIFF_V))
        outs[1].append(dv.reshape(b, s, DIFF_HEADS, DIFF_V))
        outs[2].append(_pairs_to_state(st))
        outs[3].append(mk.reshape(b, N_MEM, CROSS_HEADS, CROSS_HD))
        outs[4].append(mv.reshape(b, N_MEM, CROSS_HEADS, CROSS_HD))

        dq, dk, dv, dk16, dv16, gq, gk, gv, la, gr = _proj_in(hs.reshape(db * t, D_MODEL), p)
        seq = lambda a: a.reshape(db, t, a.shape[-1])
        od = _diff_sample(lam, page_table, seq(dq), seq(dk16), seq(dv16), cache_diff_k[l],
                          cache_diff_v[l], p)
        tp = -(-t // GLA_SUB) * GLA_SUB
        padt = lambda a: jnp.pad(seq(a), ((0, 0), (0, tp - t), (0, 0)))
        og, st = _gla(padt(gq), padt(gk), padt(gv), padt(la), _state_to_pairs(state_gla[l]), p)
        hs_pad = jnp.pad(hs, ((0, 0), (0, tp - t), (0, 0)))
        y = _tail(jnp.pad(od, ((0, 0), (0, tp - t), (0, 0))).astype(BF16), og, padt(gr), hs_pad,
                  cache_mem_k[l].reshape(db, N_MEM, D_MODEL),
                  cache_mem_v[l].reshape(db, N_MEM, D_MODEL), p)
        hs = y[:, :t]
        outs[5].append(dk.reshape(db, t, DIFF_HEADS, DIFF_V))
        outs[6].append(dv.reshape(db, t, DIFF_HEADS, DIFF_V))
        outs[7].append(_pairs_to_state(st))

    stack = lambda xs: jnp.stack(xs)
    return (hp, hs, stack(outs[0]), stack(outs[1]), stack(outs[2]), stack(outs[3]), stack(outs[4]),
            stack(outs[5]), stack(outs[6]), stack(outs[7]))
```

```python
import functools
import math

import jax
import jax.numpy as jnp
from jax import lax
from jax.experimental import pallas as pl
from jax.experimental.pallas import tpu as pltpu

F32 = jnp.float32
BF16 = jnp.bfloat16
I32 = jnp.int32

EPS = 1e-6
D_MODEL = 1024
DIFF_HEADS = 8
DIFF_QK = 32
DIFF_V = 64
DIFF_W = DIFF_HEADS * DIFF_V
GLA_HEADS = 4
GLA_DK = 64
GLA_DV = 128
GLA_QW = GLA_HEADS * GLA_DK
GLA_VW = GLA_HEADS * GLA_DV
GLA_GATE_RANK = 16
GLA_TAU = 16.0
GLA_SUB = 16
CROSS_HEADS = 4
CROSS_HD = 256
N_MEM = 256
PEER_HEADS = 8
PEER_QD = 256
N_KEYS = 128
PEER_TOPK = 16
PEER_SEL = PEER_HEADS * PEER_TOPK
PAGE_SIZE = 128
LANES = 128
VMEM_LIMIT = 56 * 1024 * 1024

_OFF_GA = 2560
_OFF_GR = 2576


def _params(sem, vmem=None):
    return pltpu.CompilerParams(dimension_semantics=sem, vmem_limit_bytes=vmem)


def _rms(x, g):
    return x * lax.rsqrt(jnp.mean(x * x, axis=-1, keepdims=True) + EPS) * g


def _split_dot(x, w):
    hi = x.astype(BF16)
    lo = (x - hi.astype(F32)).astype(BF16)
    return (jnp.dot(hi, w, preferred_element_type=F32)
            + jnp.dot(lo, w, preferred_element_type=F32))


def _group_rms(x, bd, g):
    ms = _split_dot(x * x, bd)
    return x * lax.rsqrt(ms + EPS) * g


def _block_diag(width, group, value, dtype=BF16):
    i = jnp.arange(width) // group
    return jnp.where(i[:, None] == i[None, :], value, 0.0).astype(dtype)


def _proj_in_kernel(x_ref, g_ref, wa_ref, wgr_ref, wga_ref, qn_ref, kn_ref, bd_ref, wg_ref, bg_ref,
                    dq_ref, dk_ref, dv_ref, dk16_ref, dv16_ref, gq_ref, gk_ref, gv_ref, la_ref,
                    gr_ref):
    xb = _rms(x_ref[...], g_ref[...]).astype(BF16)
    ya = jnp.dot(xb, wa_ref[...], preferred_element_type=F32)
    bd = bd_ref[...]
    dq = _group_rms(ya[:, 0:512], bd, qn_ref[...])
    dk = _group_rms(ya[:, 512:1024], bd, kn_ref[...])
    dv = ya[:, 1024:1536]
    dq_ref[...] = dq.astype(BF16)
    dk_ref[...] = dk
    dv_ref[...] = dv
    dk16_ref[...] = dk.astype(BF16)
    dv16_ref[...] = dv.astype(BF16)
    gq_ref[...] = ya[:, 1536:1792] * (GLA_DK ** -0.5)
    gk_ref[...] = ya[:, 1792:2048]
    gv_ref[...] = ya[:, 2048:2560]
    gr_ref[...] = jnp.dot(xb, wgr_ref[...], preferred_element_type=F32)
    ga = jnp.dot(xb, wga_ref[...], preferred_element_type=F32)
    z = jnp.dot(ga.astype(BF16), wg_ref[...], preferred_element_type=F32) + bg_ref[...]
    log_sig = jnp.minimum(z, 0.0) - jnp.log(1.0 + jnp.exp(-jnp.abs(z)))
    la_ref[...] = log_sig * (1.0 / GLA_TAU)


def _proj_in(x, p, tm=256):
    n = x.shape[0]
    tm = min(tm, n)
    row = lambda w: pl.BlockSpec((tm, w), lambda i: (i, 0))
    full = lambda a: pl.BlockSpec(a.shape, lambda i: (0,) * a.ndim)
    consts = (p["attn_norm"], p["w_a"], p["w_gr"], p["w_ga"], p["dq_norm"], p["dk_norm"],
              p["bd32"], p["w_gate"], p["b_gate"])
    out_w = ((512, BF16), (512, F32), (512, F32), (512, BF16), (512, BF16), (256, F32), (256, F32),
             (512, F32), (256, F32), (512, F32))
    return pl.pallas_call(
        _proj_in_kernel,
        grid=(n // tm,),
        in_specs=[row(D_MODEL)] + [full(a) for a in consts],
        out_specs=[row(w) for w, _ in out_w],
        out_shape=[jax.ShapeDtypeStruct((n, w), dt) for w, dt in out_w],
        compiler_params=_params(("parallel",), VMEM_LIMIT),
        name="proj_in",
    )(x, *consts)


def _mix_out_kernel(od_ref, og_ref, gr_ref, x_ref, gn_ref, wod_ref, wog_ref, cn_ref, wq_ref,
                    qn_ref, h_ref, qc_ref):
    og = og_ref[...]
    gr = gr_ref[...]
    gn = gn_ref[...]
    parts = []
    for h in range(GLA_HEADS):
        sl = slice(h * GLA_DV, (h + 1) * GLA_DV)
        parts.append(_rms(og[:, sl], gn[:, sl]))
    ogn = jnp.concatenate(parts, axis=-1) * (gr * jax.nn.sigmoid(gr))
    mix = (jnp.dot(od_ref[...], wod_ref[...], preferred_element_type=F32)
           + jnp.dot(ogn.astype(BF16), wog_ref[...], preferred_element_type=F32))
    h1 = x_ref[...] + mix
    h_ref[...] = h1
    hn = _rms(h1, cn_ref[...]).astype(BF16)
    q = jnp.dot(hn, wq_ref[...], preferred_element_type=F32)
    qn = qn_ref[...]
    parts = []
    for h in range(CROSS_HEADS):
        sl = slice(h * CROSS_HD, (h + 1) * CROSS_HD)
        parts.append(_rms(q[:, sl], qn[:, sl]))
    qc_ref[...] = jnp.concatenate(parts, axis=-1).astype(BF16)


def _mix_out(od, og, gr, x, p, tm=256):
    n = x.shape[0]
    tm = min(tm, n)
    row = lambda w: pl.BlockSpec((tm, w), lambda i: (i, 0))
    full = lambda a: pl.BlockSpec(a.shape, lambda i: (0,) * a.ndim)
    consts = (p["gla_out_norm"], p["w_o_d"], p["w_o_g"], p["cross_norm"], p["cross_w_q"],
              p["cross_q_norm"])
    return pl.pallas_call(
        _mix_out_kernel,
        grid=(n // tm,),
        in_specs=[row(512), row(512), row(512), row(D_MODEL)] + [full(a) for a in consts],
        out_specs=[row(D_MODEL), row(D_MODEL)],
        out_shape=[jax.ShapeDtypeStruct((n, D_MODEL), F32), jax.ShapeDtypeStruct((n, D_MODEL), BF16)],
        compiler_params=_params(("parallel",), VMEM_LIMIT),
        name="mix_out",
    )(od, og, gr, x, *consts)


def _mem_kv_kernel(m_ref, mn_ref, wkv_ref, kn_ref, k_ref, v_ref):
    mb = _rms(m_ref[...], mn_ref[...]).astype(BF16)
    kv = jnp.dot(mb, wkv_ref[...], preferred_element_type=F32)
    kn = kn_ref[...]
    parts = []
    for h in range(CROSS_HEADS):
        sl = slice(h * CROSS_HD, (h + 1) * CROSS_HD)
        parts.append(_rms(kv[:, sl], kn[:, sl]))
    k_ref[...] = jnp.concatenate(parts, axis=-1)
    v_ref[...] = kv[:, D_MODEL:]


def _mem_kv(mem, p, tm=256):
    n = mem.shape[0]
    tm = min(tm, n)
    row = lambda w: pl.BlockSpec((tm, w), lambda i: (i, 0))
    full = lambda a: pl.BlockSpec(a.shape, lambda i: (0,) * a.ndim)
    consts = (p["mem_norm"], p["cross_w_kv"], p["cross_k_norm"])
    return pl.pallas_call(
        _mem_kv_kernel,
        grid=(n // tm,),
        in_specs=[row(D_MODEL)] + [full(a) for a in consts],
        out_specs=[row(D_MODEL), row(D_MODEL)],
        out_shape=[jax.ShapeDtypeStruct((n, D_MODEL), F32)] * 2,
        compiler_params=_params(("parallel",), VMEM_LIMIT),
        name="mem_kv",
    )(mem, *consts)


def _cross_kernel(qc_ref, mk_ref, mv_ref, h1_ref, wo_ref, fn_ref, wpq_ref, pqn_ref,
                  h2_ref, xt_ref, pq_ref):
    qc = qc_ref[0]
    mk = mk_ref[0].astype(BF16)
    mv = mv_ref[0].astype(BF16)
    parts = []
    for h in range(CROSS_HEADS):
        sl = slice(h * CROSS_HD, (h + 1) * CROSS_HD)
        s = lax.dot_general(qc[:, sl], mk[:, sl], (((1,), (1,)), ((), ())),
                            preferred_element_type=F32)
        e = jnp.exp(s - jnp.max(s, axis=-1, keepdims=True))
        prob = e / jnp.sum(e, axis=-1, keepdims=True)
        parts.append(jnp.dot(prob.astype(BF16), mv[:, sl], preferred_element_type=F32))
    o = jnp.concatenate(parts, axis=-1).astype(BF16)
    h2 = h1_ref[0] + jnp.dot(o, wo_ref[...], preferred_element_type=F32)
    h2_ref[0] = h2
    xt = _rms(h2, fn_ref[...]).astype(BF16)
    order = [half * 4 + r for r in range(4) for half in range(2)] * 2
    xt_ref[0] = jnp.concatenate([xt[:, c * LANES:(c + 1) * LANES] for c in order], axis=-1)
    pq = jnp.dot(xt, wpq_ref[...], preferred_element_type=F32)
    pqn = pqn_ref[...]
    parts = []
    for h in range(PEER_HEADS):
        sl = slice(h * PEER_QD, (h + 1) * PEER_QD)
        parts.append(_rms(pq[:, sl], pqn[:, sl]))
    pq_ref[0] = jnp.concatenate(parts, axis=-1).astype(BF16)


def _cross(qc, mk, mv, h1, p, tq=256):
    b, t, _ = qc.shape
    tq = min(tq, t)
    row = lambda w: pl.BlockSpec((1, tq, w), lambda i, j: (i, j, 0))
    mem = pl.BlockSpec((1, N_MEM, D_MODEL), lambda i, j: (i, 0, 0))
    full = lambda a: pl.BlockSpec(a.shape, lambda i, j: (0,) * a.ndim)
    consts = (p["cross_w_o"], p["ffn_norm"], p["peer_w_q"], p["peer_q_norm"])
    return pl.pallas_call(
        _cross_kernel,
        grid=(b, t // tq),
        in_specs=[row(D_MODEL), mem, mem, row(D_MODEL)] + [full(a) for a in consts],
        out_specs=[row(D_MODEL), row(2048), row(2048)],
        out_shape=[jax.ShapeDtypeStruct((b, t, D_MODEL), F32),
                   jax.ShapeDtypeStruct((b, t, 2048), BF16),
                   jax.ShapeDtypeStruct((b, t, 2048), BF16)],
        compiler_params=_params(("parallel", "parallel"), VMEM_LIMIT),
        name="cross_attn",
    )(qc, mk, mv, h1, *consts)


def _diff_prompt_kernel(lam_ref, qt_ref, k_ref, vt_ref, on_ref, o_ref, m_scr, l_scr, acc_scr, *, tq):
    qi = pl.program_id(1)
    lam = lam_ref[0, 0]
    cols = 4 * tq
    chan = lax.broadcasted_iota(I32, (LANES, cols), 0)
    my_map = chan // DIFF_QK == lax.broadcasted_iota(I32, (LANES, cols), 1) // tq
    causal = (lax.broadcasted_iota(I32, (tq, cols), 0)
              <= lax.broadcasted_iota(I32, (tq, cols), 1) % tq)
    top_head = lax.broadcasted_iota(I32, (LANES, tq), 0) < DIFF_V
    for p in range(DIFF_HEADS // 2):
        sl = slice(p * LANES, (p + 1) * LANES)
        qtp = qt_ref[0, sl, :]
        qs = jnp.where(my_map, jnp.concatenate([qtp] * 4, axis=1), jnp.zeros((LANES, cols), BF16))
        m_scr[...] = jnp.full((1, cols), -jnp.inf, F32)
        l_scr[...] = jnp.zeros((1, cols), F32)
        acc_scr[...] = jnp.zeros((LANES, cols), F32)

        def block(kb, masked, qs=qs):
            t0 = pl.multiple_of(kb * tq, tq)
            s = jnp.dot(k_ref[0, pl.ds(t0, tq), sl], qs, preferred_element_type=F32)
            if masked:
                s = jnp.where(causal, s, -jnp.inf)
            m_old = m_scr[...]
            m_new = jnp.maximum(m_old, jnp.max(s, axis=0, keepdims=True))
            alpha = jnp.exp(m_old - m_new)
            e = jnp.exp(s - m_new)
            m_scr[...] = m_new
            l_scr[...] = alpha * l_scr[...] + jnp.sum(e, axis=0, keepdims=True)
            acc_scr[...] = alpha * acc_scr[...] + jnp.dot(
                vt_ref[0, sl, pl.ds(t0, tq)], e.astype(BF16), preferred_element_type=F32)

        def body(kb, carry):
            block(kb, False)
            return carry

        lax.fori_loop(0, qi, body, 0)
        block(qi, True)
        on = acc_scr[...] / l_scr[...]
        ot = jnp.where(top_head, on[:, :tq] - lam * on[:, tq:2 * tq],
                       on[:, 2 * tq:3 * tq] - lam * on[:, 3 * tq:])
        sq = ot * ot
        ms = jnp.where(top_head, jnp.sum(sq[:DIFF_V], axis=0, keepdims=True),
                       jnp.sum(sq[DIFF_V:], axis=0, keepdims=True)) * (1.0 / DIFF_V)
        ot = ot * lax.rsqrt(ms + EPS) * on_ref[sl, :]
        o_ref[0, :, sl] = ot.T.astype(BF16)


def _diff_prompt(lam, dq, dk16, dv16, p, tq=512):
    b, s, _ = dq.shape
    tq = min(tq, s)
    qt = jnp.swapaxes(dq, 1, 2)
    vt = jnp.swapaxes(dv16, 1, 2)
    return pl.pallas_call(
        functools.partial(_diff_prompt_kernel, tq=tq),
        grid=(b, s // tq),
        in_specs=[pl.BlockSpec(memory_space=pltpu.SMEM),
                  pl.BlockSpec((1, DIFF_W, tq), lambda i, j: (i, 0, j)),
                  pl.BlockSpec((1, s, DIFF_W), lambda i, j: (i, 0, 0)),
                  pl.BlockSpec((1, DIFF_W, s), lambda i, j: (i, 0, 0)),
                  pl.BlockSpec((DIFF_W, 1), lambda i, j: (0, 0))],
        out_specs=pl.BlockSpec((1, tq, DIFF_W), lambda i, j: (i, j, 0)),
        out_shape=jax.ShapeDtypeStruct((b, s, DIFF_W), BF16),
        scratch_shapes=[pltpu.VMEM((1, 4 * tq), F32), pltpu.VMEM((1, 4 * tq), F32),
                        pltpu.VMEM((LANES, 4 * tq), F32)],
        compiler_params=_params(("parallel", "parallel"), VMEM_LIMIT),
        name="diff_prompt",
    )(lam, qt, dk16, vt, p["diff_out_norm"].reshape(DIFF_W, 1))


def _gla_kernel(q_ref, k_ref, v_ref, la_ref, st0_ref, tri_ref, ones_ref, o_ref, stout_ref,
                st_scr, b_scr, *, ts, nb):
    si = pl.program_id(1)

    @pl.when(si == 0)
    def _():
        st_scr[...] = st0_ref[...]

    tri = tri_ref[...]
    ones = ones_ref[...]
    sub_q = lax.broadcasted_iota(I32, (ts, GLA_QW), 0) % GLA_SUB
    sub_v = lax.broadcasted_iota(I32, (ts, GLA_VW), 0) % GLA_SUB
    for s in range(nb):
        la = la_ref[s]
        hi = la.astype(BF16)
        r1 = la - hi.astype(F32)
        mid = r1.astype(BF16)
        lo = (r1 - mid.astype(F32)).astype(BF16)
        b = (jnp.dot(tri, hi, preferred_element_type=F32)
             + jnp.dot(tri, mid, preferred_element_type=F32)
             + jnp.dot(tri, lo, preferred_element_type=F32))
        b_scr[s] = b

        q = q_ref[s]
        k = k_ref[s]
        v = v_ref[s]
        o_intra = jnp.zeros((ts, GLA_VW), F32)
        for d in range(GLA_SUB):
            if d == 0:
                kd, bd, vd = k, b, v
            else:
                kd = pltpu.roll(k, d, 0)
                bd = pltpu.roll(b, d, 0)
                vd = pltpu.roll(v, d, 0)
            ok = sub_q >= d
            e = jnp.where(ok, q * kd * jnp.exp(jnp.where(ok, b - bd, 0.0)), 0.0)
            w = jnp.dot(e.astype(BF16), ones, preferred_element_type=F32)
            o_intra = o_intra + w * jnp.where(sub_v >= d, vd, 0.0)
        o_ref[s] = o_intra

    lane = lax.broadcasted_iota(I32, (GLA_SUB, LANES), 1)
    lane_sq = lax.broadcasted_iota(I32, (LANES, LANES), 1)

    def step(c, carry):
        t0 = pl.multiple_of(c * GLA_SUB, GLA_SUB)
        for s in range(nb):
            bc = b_scr[s, pl.ds(t0, GLA_SUB), :]
            qc = q_ref[s, pl.ds(t0, GLA_SUB), :]
            kc = k_ref[s, pl.ds(t0, GLA_SUB), :]
            vc = v_ref[s, pl.ds(t0, GLA_SUB), :]
            bl = bc[GLA_SUB - 1:GLA_SUB, :]
            qe = qc * jnp.exp(bc)
            ke = (kc * jnp.exp(bl - bc)).astype(BF16)
            dec = jnp.exp(bl)
            o_parts = []
            for p in range(GLA_HEADS // 2):
                sl = slice(p * LANES, (p + 1) * LANES)
                st = st_scr[s, p]
                stb = st.astype(BF16)
                qp = qe[:, sl]
                upd = []
                for hh in range(2):
                    mine = (lane < GLA_DK) if hh == 0 else (lane >= GLA_DK)
                    qm = jnp.where(mine, qp, 0.0).astype(BF16)
                    o_parts.append(lax.dot_general(qm, stb, (((1,), (1,)), ((), ())),
                                                   preferred_element_type=F32))
                    vh = vc[:, (2 * p + hh) * GLA_DV:(2 * p + hh + 1) * GLA_DV]
                    upd.append(jnp.dot(vh.T.astype(BF16), ke[:, sl], preferred_element_type=F32))
                st_scr[s, p] = dec[:, sl] * st + jnp.where(lane_sq < GLA_DK, upd[0], upd[1])
            o_ref[s, pl.ds(t0, GLA_SUB), :] += jnp.concatenate(o_parts, axis=-1)
        return carry

    lax.fori_loop(0, ts // GLA_SUB, step, 0)

    @pl.when(si == pl.num_programs(1) - 1)
    def _():
        stout_ref[...] = st_scr[...]


def _gla(gq, gk, gv, la, st0, p, ts=256):
    b, t, _ = gq.shape
    ts = min(ts, t)
    nb = 2 if b % 2 == 0 else 1
    i = jnp.arange(ts)
    tri = ((i[:, None] // GLA_SUB == i[None, :] // GLA_SUB) & (i[None, :] <= i[:, None])).astype(BF16)
    ones = _block_diag_rect(GLA_QW, GLA_DK, GLA_VW, GLA_DV)
    seq = lambda w: pl.BlockSpec((nb, ts, w), lambda bi, si: (bi, si, 0))
    stspec = pl.BlockSpec((nb, GLA_HEADS // 2, LANES, LANES), lambda bi, si: (bi, 0, 0, 0))
    return pl.pallas_call(
        functools.partial(_gla_kernel, ts=ts, nb=nb),
        grid=(b // nb, t // ts),
        in_specs=[seq(GLA_QW), seq(GLA_QW), seq(GLA_VW), seq(GLA_QW), stspec,
                  pl.BlockSpec((ts, ts), lambda bi, si: (0, 0)),
                  pl.BlockSpec((GLA_QW, GLA_VW), lambda bi, si: (0, 0))],
        out_specs=[seq(GLA_VW), stspec],
        out_shape=[jax.ShapeDtypeStruct((b, t, GLA_VW), F32),
                   jax.ShapeDtypeStruct((b, GLA_HEADS // 2, LANES, LANES), F32)],
        scratch_shapes=[pltpu.VMEM((nb, GLA_HEADS // 2, LANES, LANES), F32),
                        pltpu.VMEM((nb, ts, GLA_QW), F32)],
        compiler_params=_params(("parallel", "arbitrary"), VMEM_LIMIT),
        name="gla",
    )(gq, gk, gv, la, st0, tri, ones)


PAGES_PER_STEP = 8
NEW_T = 8


def _diff_sample_kernel(pt_ref, lam_ref, q_ref, *rest):
    del pt_ref
    np_ = PAGES_PER_STEP
    k_refs = rest[:np_]
    v_refs = rest[np_:2 * np_]
    kn_ref, vn_ref, on_ref, o_ref, m_scr, l_scr, acc_scr = rest[2 * np_:]
    i = pl.program_id(1)
    nt = (((1,), (1,)), ((), ()))

    @pl.when(i == 0)
    def _():
        m_scr[...] = jnp.full(m_scr.shape, -jnp.inf, F32)
        l_scr[...] = jnp.zeros(l_scr.shape, F32)
        acc_scr[...] = jnp.zeros(acc_scr.shape, F32)

    q = q_ref[0]

    def update(scores, value_products):
        m_old = m_scr[...]
        m_new = m_old
        for s in scores:
            m_new = jnp.maximum(m_new, jnp.max(s, axis=-1, keepdims=True))
        alpha = jnp.exp(m_old - m_new)
        l = alpha * l_scr[...]
        acc = alpha * acc_scr[...]
        for s, pv in zip(scores, value_products):
            e = jnp.exp(s - m_new)
            l = l + jnp.sum(e, axis=-1, keepdims=True)
            acc = acc + pv(e.astype(BF16))
        m_scr[...] = m_new
        l_scr[...] = l
        acc_scr[...] = acc

    def page_t(ref):
        return ref[...].reshape(DIFF_W, PAGE_SIZE).astype(BF16)

    scores = [jnp.dot(q, page_t(k_ref), preferred_element_type=F32) for k_ref in k_refs]
    update(scores, [lambda e, v_ref=v_ref: lax.dot_general(e, page_t(v_ref), nt,
                                                           preferred_element_type=F32)
                    for v_ref in v_refs])

    @pl.when(i == pl.num_programs(1) - 1)
    def _():
        s = lax.dot_general(q, kn_ref[0], nt, preferred_element_type=F32)
        t_q = lax.broadcasted_iota(I32, s.shape, 0) % NEW_T
        t_k = lax.broadcasted_iota(I32, s.shape, 1)
        update([jnp.where(t_k <= t_q, s, -jnp.inf)],
               [lambda e: jnp.dot(e, vn_ref[0], preferred_element_type=F32)])
        accn = acc_scr[...] / l_scr[...]
        rows = 2 * NEW_T
        lane = lax.broadcasted_iota(I32, (rows, DIFF_W), 1)
        o16 = jnp.zeros((rows, DIFF_W), F32)
        for h in range(DIFF_HEADS):
            o16 = jnp.where(lane // DIFF_V == h, accn[h * rows:(h + 1) * rows, :], o16)
        o = o16[:NEW_T] - lam_ref[0, 0] * o16[NEW_T:]
        lane = lax.broadcasted_iota(I32, (NEW_T, DIFF_W), 1)
        sq = o * o
        ms = jnp.zeros_like(o)
        for h in range(DIFF_HEADS):
            mine = lane // DIFF_V == h
            ms = jnp.where(mine, jnp.sum(jnp.where(mine, sq, 0.0), axis=-1, keepdims=True), ms)
        o_ref[0] = o * lax.rsqrt(ms * (1.0 / DIFF_V) + EPS) * on_ref[...]


def _diff_sample(lam, page_table, dq, dk16, dv16, cache_k, cache_v, p):
    db, t, _ = dq.shape
    n_pages = page_table.shape[1]
    np_ = PAGES_PER_STEP
    a = dq.reshape(db, t, 16, DIFF_QK).transpose(0, 2, 1, 3)
    q_bd = (a[:, :, :, None, :] * jnp.eye(16, dtype=BF16)[None, :, None, :, None]
            ).reshape(db, 16 * t, DIFF_W)
    pad = lambda x: jnp.pad(x, ((0, 0), (0, LANES - t), (0, 0)))
    ck = cache_k.transpose(0, 2, 3, 1)
    cv = cache_v.transpose(0, 2, 3, 1)

    def page_spec(n):
        return pl.BlockSpec((None, DIFF_HEADS, DIFF_V, PAGE_SIZE),
                            lambda b, i, pt: (pt[b, i * np_ + n], 0, 0, 0))

    per_b = lambda r: pl.BlockSpec((1, r, DIFF_W), lambda b, i, pt: (b, 0, 0))
    grid_spec = pltpu.PrefetchScalarGridSpec(
        num_scalar_prefetch=1,
        grid=(db, n_pages // np_),
        in_specs=[pl.BlockSpec(memory_space=pltpu.SMEM), per_b(16 * t)]
        + [page_spec(n) for n in range(np_)] + [page_spec(n) for n in range(np_)]
        + [per_b(LANES), per_b(LANES), pl.BlockSpec((1, DIFF_W), lambda b, i, pt: (0, 0))],
        out_specs=per_b(t),
        scratch_shapes=[pltpu.VMEM((16 * t, 1), F32), pltpu.VMEM((16 * t, 1), F32),
                        pltpu.VMEM((16 * t, DIFF_W), F32)],
    )
    return pl.pallas_call(
        _diff_sample_kernel,
        grid_spec=grid_spec,
        out_shape=jax.ShapeDtypeStruct((db, t, DIFF_W), F32),
        compiler_params=_params(("parallel", "arbitrary"), VMEM_LIMIT),
        name="diff_sample",
    )(page_table, lam, q_bd, *([ck] * np_), *([cv] * np_), pad(dk16), pad(dv16),
      p["diff_out_norm"])


def _top16_rows(s, row):
    vals, idxs = [], []
    sentinel = s.shape[0]
    for _ in range(PEER_TOPK):
        m = jnp.max(s, axis=0, keepdims=True)
        idx = jnp.min(jnp.where(s == m, row, sentinel), axis=0, keepdims=True)
        vals.append(m)
        idxs.append(idx)
        s = jnp.where(row == idx, -jnp.inf, s)
    return jnp.concatenate(vals, axis=0), jnp.concatenate(idxs, axis=0)


def _pick_row(table, row, idx):
    return jnp.max(jnp.where(row == idx, table, -1), axis=0, keepdims=True)


_CAND_WIDTH = [PEER_TOPK // (a + 1) for a in range(PEER_TOPK)]
_CAND_ROWS = -(-sum(_CAND_WIDTH) // 8) * 8


def _route_kernel(pq_ref, sk_ref, e_ref, g_ref, *, tb):
    nt = (((1,), (1,)), ((), ()))
    row_k = lax.broadcasted_iota(I32, (N_KEYS, tb), 0)
    row_c = lax.broadcasted_iota(I32, (_CAND_ROWS, tb), 0)
    n_pad = _CAND_ROWS - sum(_CAND_WIDTH)
    e_rows, g_rows = [], []
    for h in range(PEER_HEADS):
        tops = []
        for c in range(2):
            hc = 2 * h + c
            s = lax.dot_general(sk_ref[hc], pq_ref[:, hc * N_KEYS:(hc + 1) * N_KEYS], nt,
                                preferred_element_type=F32)
            tops.append(_top16_rows(s, row_k))
        (s1, i1), (s2, i2) = tops
        cand = jnp.concatenate(
            [jnp.broadcast_to(s1[a:a + 1], (nb, tb)) + s2[:nb] for a, nb in enumerate(_CAND_WIDTH)]
            + [jnp.full((n_pad, tb), -jnp.inf, F32)], axis=0)
        cand_e = jnp.concatenate(
            [jnp.broadcast_to(i1[a:a + 1], (nb, tb)) * N_KEYS + i2[:nb]
             for a, nb in enumerate(_CAND_WIDTH)] + [jnp.zeros((n_pad, tb), I32)], axis=0)
        top_s, j = _top16_rows(cand, row_c)
        e_rows.append(jnp.concatenate(
            [_pick_row(cand_e, row_c, j[r:r + 1]) for r in range(PEER_TOPK)], axis=0))
        ex = jnp.exp(top_s - top_s[0:1])
        g_rows.append(ex / jnp.sum(ex, axis=0, keepdims=True))
    e_ref[...] = (jnp.concatenate(e_rows, axis=0) * HALF_ROWS).T
    g_ref[...] = jnp.concatenate(g_rows, axis=0).T


def _route(pq, p, tb=128):
    n = pq.shape[0]
    tb = min(tb, n)
    return pl.pallas_call(
        functools.partial(_route_kernel, tb=tb),
        grid=(n // tb,),
        in_specs=[pl.BlockSpec((tb, 2048), lambda i: (i, 0)),
                  pl.BlockSpec((2 * PEER_HEADS, N_KEYS, N_KEYS), lambda i: (0, 0, 0))],
        out_specs=[pl.BlockSpec((tb, PEER_SEL), lambda i: (i, 0))] * 2,
        out_shape=[jax.ShapeDtypeStruct((n, PEER_SEL), I32), jax.ShapeDtypeStruct((n, PEER_SEL), F32)],
        compiler_params=_params(("parallel",), VMEM_LIMIT),
        name="peer_route",
    )(pq, p["peer_sub_keys"])


HALF_ROWS = 4


def _table_row(tab_ref, row_start):
    return tab_ref[pl.ds(pl.multiple_of(row_start, HALF_ROWS), HALF_ROWS), :]


def _unpack_pair(row):
    lo = lax.bitcast_convert_type(row << 16, F32)
    hi = lax.bitcast_convert_type(row & jnp.uint32(0xFFFF0000), F32)
    return lo, hi


def _gelu_tanh(x):
    return 0.5 * x * (1.0 + jnp.tanh(math.sqrt(2.0 / math.pi) * (x + 0.044715 * (x * x * x))))


def _gather_rows(e_ref, tab_ref, t, stage):
    for j in range(PEER_SEL):
        stage[j * HALF_ROWS:(j + 1) * HALF_ROWS, :] = _table_row(tab_ref, e_ref[t, j])


def _peer_dot_kernel(e_ref, xt_ref, g_ref, tab_ref, gsum_ref, w_ref, stage0, stage1,
                     part0, part1, *, tb):
    lane = lax.broadcasted_iota(I32, (PEER_SEL, tb), 1)
    gsum = gsum_ref[...]
    chunks = 2 * HALF_ROWS
    gsum16 = gsum_ref[:LANES // chunks, :LANES]
    pairs = PEER_SEL * chunks // 16

    def group_sums(t):
        prods = []
        for k, stage in enumerate((stage0, stage1)):
            x = xt_ref[jnp.maximum(t + k, 0)]
            rows = pltpu.bitcast(stage[...], BF16).reshape(pairs, 16, LANES)
            prods.append((rows * x[None]).reshape(PEER_SEL * chunks, LANES))
        per = LANES // chunks
        for c in range(PEER_SEL // per):
            for prod, part in zip(prods, (part0, part1)):
                part[c * per:(c + 1) * per, :] = jnp.dot(
                    gsum16, prod[c * LANES:(c + 1) * LANES, :], preferred_element_type=F32)

    def lane_sums(t, part, ht):
        return jnp.where(lane == t, jnp.sum(part[...], axis=1, keepdims=True), ht)

    for ref in (stage0, stage1, part0, part1):
        ref[...] = jnp.zeros(ref.shape, ref.dtype)

    def token_pair(i, ht):
        t0 = 2 * i
        ht = lane_sums(t0 - 4, part0, ht)
        ht = lane_sums(t0 - 3, part1, ht)
        group_sums(t0 - 2)
        _gather_rows(e_ref, tab_ref, t0, stage0)
        _gather_rows(e_ref, tab_ref, t0 + 1, stage1)
        return ht

    ht = lax.fori_loop(0, tb // 2, token_pair, jnp.zeros((PEER_SEL, tb), F32))
    ht = lane_sums(tb - 4, part0, ht)
    ht = lane_sums(tb - 3, part1, ht)
    group_sums(tb - 2)
    ht = lane_sums(tb - 2, part0, ht)
    ht = lane_sums(tb - 1, part1, ht)
    w = g_ref[...] * _gelu_tanh(ht.T)
    w_ref[...] = jnp.dot(w.astype(BF16), gsum, preferred_element_type=F32)


def _peer_dot(e, xt3, g, tab, tb=128):
    n = e.shape[0]
    tb = min(tb, n)
    gsum = _block_diag_rect(PEER_SEL, 1, PEER_SEL * 2 * HALF_ROWS, 2 * HALF_ROWS)
    return pl.pallas_call(
        functools.partial(_peer_dot_kernel, tb=tb),
        grid=(n // tb,),
        in_specs=[pl.BlockSpec((tb, PEER_SEL), lambda i: (i, 0), memory_space=pltpu.SMEM),
                  pl.BlockSpec((tb, 16, LANES), lambda i: (i, 0, 0)),
                  pl.BlockSpec((tb, PEER_SEL), lambda i: (i, 0)),
                  pl.BlockSpec(memory_space=pltpu.VMEM),
                  pl.BlockSpec(gsum.shape, lambda i: (0, 0))],
        out_specs=pl.BlockSpec((tb, D_MODEL), lambda i: (i, 0)),
        out_shape=jax.ShapeDtypeStruct((n, D_MODEL), F32),
        scratch_shapes=[pltpu.VMEM((PEER_SEL * HALF_ROWS, LANES), jnp.uint32)] * 2
        + [pltpu.VMEM((PEER_SEL, LANES), F32)] * 2,
        compiler_params=_params(("arbitrary",), VMEM_LIMIT),
        name="peer_dot",
    )(e, xt3, g, tab, gsum)


def _peer_sum_kernel(e_ref, w_ref, h_ref, tab_ref, y_ref, stage0, stage1, *, tb):
    out_rows = 2 * HALF_ROWS
    q = lax.broadcasted_iota(I32, (2 * out_rows, D_MODEL), 0)
    lane = lax.broadcasted_iota(I32, (2 * out_rows, D_MODEL), 1)
    mine = (lane % out_rows == 2 * (q % HALF_ROWS) + q // HALF_ROWS) & (q < out_rows)

    def gather(t, stage):
        _gather_rows(e_ref, tab_ref, t, stage)

    def weighted_sum(t, stage):
        wrow = w_ref[pl.ds(t, 1), :]
        lhs = jnp.where(mine, wrow, 0.0).astype(BF16)
        rows = pltpu.bitcast(stage[...], BF16)
        out = jnp.dot(lhs, rows, preferred_element_type=F32)
        y_ref[t] = h_ref[t] + out[:out_rows]

    stage0[...] = jnp.zeros(stage0.shape, stage0.dtype)
    stage1[...] = jnp.zeros(stage1.shape, stage1.dtype)

    def token_pair(i, carry):
        t0 = 2 * i
        weighted_sum(jnp.maximum(t0 - 2, 0), stage0)
        weighted_sum(jnp.maximum(t0 - 1, 0), stage1)
        gather(t0, stage0)
        gather(t0 + 1, stage1)
        return carry

    lax.fori_loop(0, tb // 2, token_pair, 0)
    weighted_sum(tb - 2, stage0)
    weighted_sum(tb - 1, stage1)


def _peer_sum(e, w, h3, tab, tb=128):
    n = e.shape[0]
    tb = min(tb, n)
    tok = lambda: pl.BlockSpec((tb, 8, LANES), lambda i: (i, 0, 0))
    stage = pltpu.VMEM((PEER_SEL * HALF_ROWS, LANES), jnp.uint32)
    return pl.pallas_call(
        functools.partial(_peer_sum_kernel, tb=tb),
        grid=(n // tb,),
        in_specs=[pl.BlockSpec((tb, PEER_SEL), lambda i: (i, 0), memory_space=pltpu.SMEM),
                  pl.BlockSpec((tb, D_MODEL), lambda i: (i, 0)), tok(),
                  pl.BlockSpec(memory_space=pltpu.VMEM)],
        out_specs=tok(),
        out_shape=jax.ShapeDtypeStruct((n, 8, LANES), F32),
        scratch_shapes=[stage, stage],
        compiler_params=_params(("arbitrary",), VMEM_LIMIT),
        name="peer_sum",
    )(e, w, h3, tab)


def _pack_table(w):
    bits = lax.bitcast_convert_type(w.astype(BF16), jnp.uint16).astype(jnp.uint32)
    half = D_MODEL // 2
    return (bits[:, :half] | (bits[:, half:] << 16)).reshape(w.shape[0] * HALF_ROWS, LANES)


def _block_diag_rect(rows, rgroup, cols, cgroup):
    r = jnp.arange(rows) // rgroup
    c = jnp.arange(cols) // cgroup
    return (r[:, None] == c[None, :]).astype(BF16)


def _state_to_pairs(s):
    b = s.shape[0]
    return s.reshape(b, 2, 2, GLA_DK, GLA_DV).transpose(0, 1, 4, 2, 3).reshape(b, 2, LANES, LANES)


def _pairs_to_state(st):
    b = st.shape[0]
    return st.reshape(b, 2, GLA_DV, 2, GLA_DK).transpose(0, 1, 3, 4, 2).reshape(
        b, GLA_HEADS, GLA_DK, GLA_DV)


def _lambda_init(layer):
    return 0.8 - 0.6 * math.exp(-0.3 * layer)


def _prep_layer(l, w):
    row = lambda a: a.reshape(1, -1).astype(F32)
    w_in = w["w_in"][l]
    lam_init = _lambda_init(l)
    return {
        "attn_norm": row(w["attn_norm"][l]),
        "w_a": w_in[:, :_OFF_GA].astype(BF16),
        "w_ga": jnp.pad(w_in[:, _OFF_GA:_OFF_GR], ((0, 0), (0, LANES - GLA_GATE_RANK))).astype(BF16),
        "w_gr": w_in[:, _OFF_GR:].astype(BF16),
        "dq_norm": row(jnp.tile(w["diff_q_norm"][l], 2 * DIFF_HEADS)) * (DIFF_QK ** -0.5),
        "dk_norm": row(jnp.tile(w["diff_k_norm"][l], 2 * DIFF_HEADS)),
        "bd32": _block_diag(DIFF_W, DIFF_QK, 1.0 / DIFF_QK),
        "w_gate": jnp.pad(w["gla_w_gate"][l], ((0, LANES - GLA_GATE_RANK), (0, 0))).astype(BF16),
        "b_gate": row(w["gla_b_gate"][l]),
        "diff_out_norm": row(jnp.tile(w["diff_out_norm"][l], DIFF_HEADS)) * (1.0 - lam_init),
        "gla_out_norm": row(jnp.tile(w["gla_out_norm"][l], GLA_HEADS)),
        "w_o_d": w["w_o"][l][:DIFF_W].astype(BF16),
        "w_o_g": w["w_o"][l][DIFF_W:].astype(BF16),
        "cross_norm": row(w["cross_norm"][l]),
        "cross_w_q": w["cross_w_q"][l].astype(BF16),
        "cross_q_norm": row(jnp.tile(w["cross_q_norm"][l], CROSS_HEADS)) * (CROSS_HD ** -0.5),
        "mem_norm": row(w["mem_norm"][l]),
        "cross_w_kv": w["cross_w_kv"][l].astype(BF16),
        "cross_k_norm": row(jnp.tile(w["cross_k_norm"][l], CROSS_HEADS)),
        "cross_w_o": w["cross_w_o"][l].astype(BF16),
        "ffn_norm": row(w["ffn_norm"][l]),
        "peer_w_q": w["peer_w_q"][l].astype(BF16),
        "peer_q_norm": row(jnp.tile(w["peer_q_norm"][l], PEER_HEADS)),
        "peer_sub_keys": w["peer_sub_keys"][l].reshape(2 * PEER_HEADS, N_KEYS, N_KEYS).astype(BF16),
        "peer_u": _pack_table(w["peer_u"][l]),
        "peer_v": _pack_table(w["peer_v"][l]),
    }


def _diff_lambda(w, l):
    f = lambda a: a[l].astype(F32)
    lam = (jnp.exp(jnp.sum(f(w["lambda_q1"]) * f(w["lambda_k1"])))
           - jnp.exp(jnp.sum(f(w["lambda_q2"]) * f(w["lambda_k2"]))) + _lambda_init(l))
    return lam.reshape(1, 1)


def _peer(h2, xt, pq, p):
    n = h2.shape[0]
    e, g = _route(pq, p)
    w = _peer_dot(e, xt.reshape(n, 16, LANES), g, p["peer_u"])
    return _peer_sum(e, w, h2.reshape(n, 8, LANES), p["peer_v"]).reshape(n, D_MODEL)


def _tail(od, og, gr, x, mk, mv, p):
    b, t, _ = x.shape
    n = b * t
    flat = lambda a: a.reshape(n, a.shape[-1])
    h1, qc = _mix_out(flat(od), flat(og), flat(gr), flat(x), p)
    h2, xt, pq = _cross(qc.reshape(b, t, D_MODEL), mk, mv, h1.reshape(b, t, D_MODEL), p)
    return _peer(flat(h2), flat(xt), flat(pq), p).reshape(b, t, D_MODEL)


def kernel(x_prompt, x_sample, mem_prompt, cache_diff_k, cache_diff_v, page_table, state_gla,
           cache_mem_k, cache_mem_v, attn_norm, w_in, diff_q_norm, diff_k_norm, lambda_q1,
           lambda_k1, lambda_q2, lambda_k2, diff_out_norm, gla_w_gate, gla_b_gate, gla_out_norm,
           w_o, cross_norm, mem_norm, cross_w_q, cross_w_kv, cross_q_norm, cross_k_norm,
           cross_w_o, ffn_norm, peer_w_q, peer_q_norm, peer_sub_keys, peer_u, peer_v):
    w = dict(attn_norm=attn_norm, w_in=w_in, diff_q_norm=diff_q_norm, diff_k_norm=diff_k_norm,
             lambda_q1=lambda_q1, lambda_k1=lambda_k1, lambda_q2=lambda_q2, lambda_k2=lambda_k2,
             diff_out_norm=diff_out_norm, gla_w_gate=gla_w_gate, gla_b_gate=gla_b_gate,
             gla_out_norm=gla_out_norm, w_o=w_o, cross_norm=cross_norm, mem_norm=mem_norm,
             cross_w_q=cross_w_q, cross_w_kv=cross_w_kv, cross_q_norm=cross_q_norm,
             cross_k_norm=cross_k_norm, cross_w_o=cross_w_o, ffn_norm=ffn_norm,
             peer_w_q=peer_w_q, peer_q_norm=peer_q_norm, peer_sub_keys=peer_sub_keys,
             peer_u=peer_u, peer_v=peer_v)
    b, s, _ = x_prompt.shape
    db, t, _ = x_sample.shape
    depth = w_in.shape[0]
    hp, hs = x_prompt, x_sample
    outs = [[] for _ in range(8)]
    for l in range(depth):
        p = _prep_layer(l, w)
        lam = _diff_lambda(w, l)

        dq, dk, dv, dk16, dv16, gq, gk, gv, la, gr = _proj_in(hp.reshape(b * s, D_MODEL), p)
        seq = lambda a: a.reshape(b, s, a.shape[-1])
        od = _diff_prompt(lam, seq(dq), seq(dk16), seq(dv16), p)
        st0 = jnp.zeros((b, GLA_HEADS // 2, LANES, LANES), F32)
        og, st = _gla(seq(gq), seq(gk), seq(gv), seq(la), st0, p)
        mk, mv = _mem_kv(mem_prompt.reshape(b * N_MEM, D_MODEL), p)
        mk = mk.reshape(b, N_MEM, D_MODEL)
        mv = mv.reshape(b, N_MEM, D_MODEL)
        hp = _tail(od, og, seq(gr), hp, mk, mv, p)
        outs[0].append(dk.reshape(b, s, DIFF_HEADS, DIFF_V))
        outs[1].append(dv.reshape(b, s, DIFF_HEADS, DIFF_V))
        outs[2].append(_pairs_to_state(st))
        outs[3].append(mk.reshape(b, N_MEM, CROSS_HEADS, CROSS_HD))
        outs[4].append(mv.reshape(b, N_MEM, CROSS_HEADS, CROSS_HD))

        dq, dk, dv, dk16, dv16, gq, gk, gv, la, gr = _proj_in(hs.reshape(db * t, D_MODEL), p)
        seq = lambda a: a.reshape(db, t, a.shape[-1])
        od = _diff_sample(lam, page_table, seq(dq), seq(dk16), seq(dv16), cache_diff_k[l],
                          cache_diff_v[l], p)
        tp = -(-t // GLA_SUB) * GLA_SUB
        padt = lambda a: jnp.pad(seq(a), ((0, 0), (0, tp - t), (0, 0)))
        og, st = _gla(padt(gq), padt(gk), padt(gv), padt(la), _state_to_pairs(state_gla[l]), p)
        hs_pad = jnp.pad(hs, ((0, 0), (0, tp - t), (0, 0)))
        y = _tail(jnp.pad(od, ((0, 0), (0, tp - t), (0, 0))).astype(BF16), og, padt(gr), hs_pad,
                  cache_mem_k[l].reshape(db, N_MEM, D_MODEL),
                  cache_mem_v[l].reshape(db, N_MEM, D_MODEL), p)
        hs = y[:, :t]
        outs[5].append(dk.reshape(db, t, DIFF_HEADS, DIFF_V))
        outs[6].append(dv.reshape(db, t, DIFF_HEADS, DIFF_V))
        outs[7].append(_pairs_to_state(st))

    stack = lambda xs: jnp.stack(xs)
    return (hp, hs, stack(outs[0]), stack(outs[1]), stack(outs[2]), stack(outs[3]), stack(outs[4]),
            stack(outs[5]), stack(outs[6]), stack(outs[7]))
```

```python
import functools
import math

import jax
import jax.numpy as jnp
from jax import lax
from jax.experimental import pallas as pl
from jax.experimental.pallas import tpu as pltpu

F32 = jnp.float32
BF16 = jnp.bfloat16
I32 = jnp.int32

EPS = 1e-6
D_MODEL = 1024
DIFF_HEADS = 8
DIFF_QK = 32
DIFF_V = 64
DIFF_W = DIFF_HEADS * DIFF_V
GLA_HEADS = 4
GLA_DK = 64
GLA_DV = 128
GLA_QW = GLA_HEADS * GLA_DK
GLA_VW = GLA_HEADS * GLA_DV
GLA_GATE_RANK = 16
GLA_TAU = 16.0
GLA_SUB = 16
CROSS_HEADS = 4
CROSS_HD = 256
N_MEM = 256
PEER_HEADS = 8
PEER_QD = 256
N_KEYS = 128
PEER_TOPK = 16
PEER_SEL = PEER_HEADS * PEER_TOPK
PAGE_SIZE = 128
LANES = 128
VMEM_LIMIT = 56 * 1024 * 1024

_OFF_GA = 2560
_OFF_GR = 2576


def _params(sem, vmem=None):
    return pltpu.CompilerParams(dimension_semantics=sem, vmem_limit_bytes=vmem)


def _rms(x, g):
    return x * lax.rsqrt(jnp.mean(x * x, axis=-1, keepdims=True) + EPS) * g


def _split_dot(x, w):
    hi = x.astype(BF16)
    lo = (x - hi.astype(F32)).astype(BF16)
    return (jnp.dot(hi, w, preferred_element_type=F32)
            + jnp.dot(lo, w, preferred_element_type=F32))


def _group_rms(x, bd, g):
    ms = _split_dot(x * x, bd)
    return x * lax.rsqrt(ms + EPS) * g


def _block_diag(width, group, value, dtype=BF16):
    i = jnp.arange(width) // group
    return jnp.where(i[:, None] == i[None, :], value, 0.0).astype(dtype)


def _proj_in_kernel(x_ref, g_ref, wa_ref, wgr_ref, wga_ref, qn_ref, kn_ref, bd_ref, wg_ref, bg_ref,
                    dq_ref, dk_ref, dv_ref, dk16_ref, dv16_ref, gq_ref, gk_ref, gv_ref, la_ref,
                    gr_ref):
    xb = _rms(x_ref[...], g_ref[...]).astype(BF16)
    ya = jnp.dot(xb, wa_ref[...], preferred_element_type=F32)
    bd = bd_ref[...]
    dq = _group_rms(ya[:, 0:512], bd, qn_ref[...])
    dk = _group_rms(ya[:, 512:1024], bd, kn_ref[...])
    dv = ya[:, 1024:1536]
    dq_ref[...] = dq.astype(BF16)
    dk_ref[...] = dk
    dv_ref[...] = dv
    dk16_ref[...] = dk.astype(BF16)
    dv16_ref[...] = dv.astype(BF16)
    gq_ref[...] = ya[:, 1536:1792] * (GLA_DK ** -0.5)
    gk_ref[...] = ya[:, 1792:2048]
    gv_ref[...] = ya[:, 2048:2560]
    gr_ref[...] = jnp.dot(xb, wgr_ref[...], preferred_element_type=F32)
    ga = jnp.dot(xb, wga_ref[...], preferred_element_type=F32)
    z = jnp.dot(ga.astype(BF16), wg_ref[...], preferred_element_type=F32) + bg_ref[...]
    log_sig = jnp.minimum(z, 0.0) - jnp.log(1.0 + jnp.exp(-jnp.abs(z)))
    la_ref[...] = log_sig * (1.0 / GLA_TAU)


def _proj_in(x, p, tm=256):
    n = x.shape[0]
    tm = min(tm, n)
    row = lambda w: pl.BlockSpec((tm, w), lambda i: (i, 0))
    full = lambda a: pl.BlockSpec(a.shape, lambda i: (0,) * a.ndim)
    consts = (p["attn_norm"], p["w_a"], p["w_gr"], p["w_ga"], p["dq_norm"], p["dk_norm"],
              p["bd32"], p["w_gate"], p["b_gate"])
    out_w = ((512, BF16), (512, F32), (512, F32), (512, BF16), (512, BF16), (256, F32), (256, F32),
             (512, F32), (256, F32), (512, F32))
    return pl.pallas_call(
        _proj_in_kernel,
        grid=(n // tm,),
        in_specs=[row(D_MODEL)] + [full(a) for a in consts],
        out_specs=[row(w) for w, _ in out_w],
        out_shape=[jax.ShapeDtypeStruct((n, w), dt) for w, dt in out_w],
        compiler_params=_params(("parallel",), VMEM_LIMIT),
        name="proj_in",
    )(x, *consts)


def _mix_out_kernel(od_ref, og_ref, gr_ref, x_ref, gn_ref, wod_ref, wog_ref, cn_ref, wq_ref,
                    qn_ref, h_ref, qc_ref):
    og = og_ref[...]
    gr = gr_ref[...]
    gn = gn_ref[...]
    parts = []
    for h in range(GLA_HEADS):
        sl = slice(h * GLA_DV, (h + 1) * GLA_DV)
        parts.append(_rms(og[:, sl], gn[:, sl]))
    ogn = jnp.concatenate(parts, axis=-1) * (gr * jax.nn.sigmoid(gr))
    mix = (jnp.dot(od_ref[...], wod_ref[...], preferred_element_type=F32)
           + jnp.dot(ogn.astype(BF16), wog_ref[...], preferred_element_type=F32))
    h1 = x_ref[...] + mix
    h_ref[...] = h1
    hn = _rms(h1, cn_ref[...]).astype(BF16)
    q = jnp.dot(hn, wq_ref[...], preferred_element_type=F32)
    qn = qn_ref[...]
    parts = []
    for h in range(CROSS_HEADS):
        sl = slice(h * CROSS_HD, (h + 1) * CROSS_HD)
        parts.append(_rms(q[:, sl], qn[:, sl]))
    qc_ref[...] = jnp.concatenate(parts, axis=-1).astype(BF16)


def _mix_out(od, og, gr, x, p, tm=256):
    n = x.shape[0]
    tm = min(tm, n)
    row = lambda w: pl.BlockSpec((tm, w), lambda i: (i, 0))
    full = lambda a: pl.BlockSpec(a.shape, lambda i: (0,) * a.ndim)
    consts = (p["gla_out_norm"], p["w_o_d"], p["w_o_g"], p["cross_norm"], p["cross_w_q"],
              p["cross_q_norm"])
    return pl.pallas_call(
        _mix_out_kernel,
        grid=(n // tm,),
        in_specs=[row(512), row(512), row(512), row(D_MODEL)] + [full(a) for a in consts],
        out_specs=[row(D_MODEL), row(D_MODEL)],
        out_shape=[jax.ShapeDtypeStruct((n, D_MODEL), F32), jax.ShapeDtypeStruct((n, D_MODEL), BF16)],
        compiler_params=_params(("parallel",), VMEM_LIMIT),
        name="mix_out",
    )(od, og, gr, x, *consts)


def _mem_kv_kernel(m_ref, mn_ref, wkv_ref, kn_ref, k_ref, v_ref):
    mb = _rms(m_ref[...], mn_ref[...]).astype(BF16)
    kv = jnp.dot(mb, wkv_ref[...], preferred_element_type=F32)
    kn = kn_ref[...]
    parts = []
    for h in range(CROSS_HEADS):
        sl = slice(h * CROSS_HD, (h + 1) * CROSS_HD)
        parts.append(_rms(kv[:, sl], kn[:, sl]))
    k_ref[...] = jnp.concatenate(parts, axis=-1)
    v_ref[...] = kv[:, D_MODEL:]


def _mem_kv(mem, p, tm=256):
    n = mem.shape[0]
    tm = min(tm, n)
    row = lambda w: pl.BlockSpec((tm, w), lambda i: (i, 0))
    full = lambda a: pl.BlockSpec(a.shape, lambda i: (0,) * a.ndim)
    consts = (p["mem_norm"], p["cross_w_kv"], p["cross_k_norm"])
    return pl.pallas_call(
        _mem_kv_kernel,
        grid=(n // tm,),
        in_specs=[row(D_MODEL)] + [full(a) for a in consts],
        out_specs=[row(D_MODEL), row(D_MODEL)],
        out_shape=[jax.ShapeDtypeStruct((n, D_MODEL), F32)] * 2,
        compiler_params=_params(("parallel",), VMEM_LIMIT),
        name="mem_kv",
    )(mem, *consts)


def _cross_kernel(qc_ref, mk_ref, mv_ref, h1_ref, wo_ref, fn_ref, wpq_ref, pqn_ref,
                  h2_ref, xt_ref, pq_ref):
    qc = qc_ref[0]
    mk = mk_ref[0].astype(BF16)
    mv = mv_ref[0].astype(BF16)
    parts = []
    for h in range(CROSS_HEADS):
        sl = slice(h * CROSS_HD, (h + 1) * CROSS_HD)
        s = lax.dot_general(qc[:, sl], mk[:, sl], (((1,), (1,)), ((), ())),
                            preferred_element_type=F32)
        e = jnp.exp(s - jnp.max(s, axis=-1, keepdims=True))
        prob = e / jnp.sum(e, axis=-1, keepdims=True)
        parts.append(jnp.dot(prob.astype(BF16), mv[:, sl], preferred_element_type=F32))
    o = jnp.concatenate(parts, axis=-1).astype(BF16)
    h2 = h1_ref[0] + jnp.dot(o, wo_ref[...], preferred_element_type=F32)
    h2_ref[0] = h2
    xt = _rms(h2, fn_ref[...]).astype(BF16)
    order = [half * 4 + r for r in range(4) for half in range(2)] * 2
    xt_ref[0] = jnp.concatenate([xt[:, c * LANES:(c + 1) * LANES] for c in order], axis=-1)
    pq = jnp.dot(xt, wpq_ref[...], preferred_element_type=F32)
    pqn = pqn_ref[...]
    parts = []
    for h in range(PEER_HEADS):
        sl = slice(h * PEER_QD, (h + 1) * PEER_QD)
        parts.append(_rms(pq[:, sl], pqn[:, sl]))
    pq_ref[0] = jnp.concatenate(parts, axis=-1).astype(BF16)


def _cross(qc, mk, mv, h1, p, tq=256):
    b, t, _ = qc.shape
    tq = min(tq, t)
    row = lambda w: pl.BlockSpec((1, tq, w), lambda i, j: (i, j, 0))
    mem = pl.BlockSpec((1, N_MEM, D_MODEL), lambda i, j: (i, 0, 0))
    full = lambda a: pl.BlockSpec(a.shape, lambda i, j: (0,) * a.ndim)
    consts = (p["cross_w_o"], p["ffn_norm"], p["peer_w_q"], p["peer_q_norm"])
    return pl.pallas_call(
        _cross_kernel,
        grid=(b, t // tq),
        in_specs=[row(D_MODEL), mem, mem, row(D_MODEL)] + [full(a) for a in consts],
        out_specs=[row(D_MODEL), row(2048), row(2048)],
        out_shape=[jax.ShapeDtypeStruct((b, t, D_MODEL), F32),
                   jax.ShapeDtypeStruct((b, t, 2048), BF16),
                   jax.ShapeDtypeStruct((b, t, 2048), BF16)],
        compiler_params=_params(("parallel", "parallel"), VMEM_LIMIT),
        name="cross_attn",
    )(qc, mk, mv, h1, *consts)


def _diff_prompt_kernel(lam_ref, qt_ref, k_ref, vt_ref, on_ref, o_ref, m_scr, l_scr, acc_scr, *, tq):
    qi = pl.program_id(1)
    lam = lam_ref[0, 0]
    cols = 4 * tq
    chan = lax.broadcasted_iota(I32, (LANES, cols), 0)
    my_map = chan // DIFF_QK == lax.broadcasted_iota(I32, (LANES, cols), 1) // tq
    causal = (lax.broadcasted_iota(I32, (tq, cols), 0)
              <= lax.broadcasted_iota(I32, (tq, cols), 1) % tq)
    top_head = lax.broadcasted_iota(I32, (LANES, tq), 0) < DIFF_V
    for p in range(DIFF_HEADS // 2):
        sl = slice(p * LANES, (p + 1) * LANES)
        qtp = qt_ref[0, sl, :]
        qs = jnp.where(my_map, jnp.concatenate([qtp] * 4, axis=1), jnp.zeros((LANES, cols), BF16))
        m_scr[...] = jnp.full((1, cols), -jnp.inf, F32)
        l_scr[...] = jnp.zeros((1, cols), F32)
        acc_scr[...] = jnp.zeros((LANES, cols), F32)

        def block(kb, masked, qs=qs):
            t0 = pl.multiple_of(kb * tq, tq)
            s = jnp.dot(k_ref[0, pl.ds(t0, tq), sl], qs, preferred_element_type=F32)
            if masked:
                s = jnp.where(causal, s, -jnp.inf)
            m_old = m_scr[...]
            m_new = jnp.maximum(m_old, jnp.max(s, axis=0, keepdims=True))
            alpha = jnp.exp(m_old - m_new)
            e = jnp.exp(s - m_new)
            m_scr[...] = m_new
            l_scr[...] = alpha * l_scr[...] + jnp.sum(e, axis=0, keepdims=True)
            acc_scr[...] = alpha * acc_scr[...] + jnp.dot(
                vt_ref[0, sl, pl.ds(t0, tq)], e.astype(BF16), preferred_element_type=F32)

        def body(kb, carry):
            block(kb, False)
            return carry

        lax.fori_loop(0, qi, body, 0)
        block(qi, True)
        on = acc_scr[...] / l_scr[...]
        ot = jnp.where(top_head, on[:, :tq] - lam * on[:, tq:2 * tq],
                       on[:, 2 * tq:3 * tq] - lam * on[:, 3 * tq:])
        sq = ot * ot
        ms = jnp.where(top_head, jnp.sum(sq[:DIFF_V], axis=0, keepdims=True),
                       jnp.sum(sq[DIFF_V:], axis=0, keepdims=True)) * (1.0 / DIFF_V)
        ot = ot * lax.rsqrt(ms + EPS) * on_ref[sl, :]
        o_ref[0, :, sl] = ot.T.astype(BF16)


def _diff_prompt(lam, dq, dk16, dv16, p, tq=512):
    b, s, _ = dq.shape
    tq = min(tq, s)
    qt = jnp.swapaxes(dq, 1, 2)
    vt = jnp.swapaxes(dv16, 1, 2)
    return pl.pallas_call(
        functools.partial(_diff_prompt_kernel, tq=tq),
        grid=(b, s // tq),
        in_specs=[pl.BlockSpec(memory_space=pltpu.SMEM),
                  pl.BlockSpec((1, DIFF_W, tq), lambda i, j: (i, 0, j)),
                  pl.BlockSpec((1, s, DIFF_W), lambda i, j: (i, 0, 0)),
                  pl.BlockSpec((1, DIFF_W, s), lambda i, j: (i, 0, 0)),
                  pl.BlockSpec((DIFF_W, 1), lambda i, j: (0, 0))],
        out_specs=pl.BlockSpec((1, tq, DIFF_W), lambda i, j: (i, j, 0)),
        out_shape=jax.ShapeDtypeStruct((b, s, DIFF_W), BF16),
        scratch_shapes=[pltpu.VMEM((1, 4 * tq), F32), pltpu.VMEM((1, 4 * tq), F32),
                        pltpu.VMEM((LANES, 4 * tq), F32)],
        compiler_params=_params(("parallel", "parallel"), VMEM_LIMIT),
        name="diff_prompt",
    )(lam, qt, dk16, vt, p["diff_out_norm"].reshape(DIFF_W, 1))


def _gla_kernel(q_ref, k_ref, v_ref, la_ref, st0_ref, tri_ref, ones_ref, o_ref, stout_ref,
                st_scr, b_scr, *, ts, nb):
    si = pl.program_id(1)

    @pl.when(si == 0)
    def _():
        st_scr[...] = st0_ref[...]

    tri = tri_ref[...]
    ones = ones_ref[...]
    sub_q = lax.broadcasted_iota(I32, (ts, GLA_QW), 0) % GLA_SUB
    sub_v = lax.broadcasted_iota(I32, (ts, GLA_VW), 0) % GLA_SUB
    for s in range(nb):
        la = la_ref[s]
        hi = la.astype(BF16)
        r1 = la - hi.astype(F32)
        mid = r1.astype(BF16)
        lo = (r1 - mid.astype(F32)).astype(BF16)
        b = (jnp.dot(tri, hi, preferred_element_type=F32)
             + jnp.dot(tri, mid, preferred_element_type=F32)
             + jnp.dot(tri, lo, preferred_element_type=F32))
        b_scr[s] = b

        q = q_ref[s]
        k = k_ref[s]
        v = v_ref[s]
        o_intra = jnp.zeros((ts, GLA_VW), F32)
        for d in range(GLA_SUB):
            if d == 0:
                kd, bd, vd = k, b, v
            else:
                kd = pltpu.roll(k, d, 0)
                bd = pltpu.roll(b, d, 0)
                vd = pltpu.roll(v, d, 0)
            ok = sub_q >= d
            e = jnp.where(ok, q * kd * jnp.exp(jnp.where(ok, b - bd, 0.0)), 0.0)
            w = jnp.dot(e.astype(BF16), ones, preferred_element_type=F32)
            o_intra = o_intra + w * jnp.where(sub_v >= d, vd, 0.0)
        o_ref[s] = o_intra

    lane = lax.broadcasted_iota(I32, (GLA_SUB, LANES), 1)
    lane_sq = lax.broadcasted_iota(I32, (LANES, LANES), 1)

    def step(c, carry):
        t0 = pl.multiple_of(c * GLA_SUB, GLA_SUB)
        for s in range(nb):
            bc = b_scr[s, pl.ds(t0, GLA_SUB), :]
            qc = q_ref[s, pl.ds(t0, GLA_SUB), :]
            kc = k_ref[s, pl.ds(t0, GLA_SUB), :]
            vc = v_ref[s, pl.ds(t0, GLA_SUB), :]
            bl = bc[GLA_SUB - 1:GLA_SUB, :]
            qe = qc * jnp.exp(bc)
            ke = (kc * jnp.exp(bl - bc)).astype(BF16)
            dec = jnp.exp(bl)
            o_parts = []
            for p in range(GLA_HEADS // 2):
                sl = slice(p * LANES, (p + 1) * LANES)
                st = st_scr[s, p]
                stb = st.astype(BF16)
                qp = qe[:, sl]
                upd = []
                for hh in range(2):
                    mine = (lane < GLA_DK) if hh == 0 else (lane >= GLA_DK)
                    qm = jnp.where(mine, qp, 0.0).astype(BF16)
                    o_parts.append(lax.dot_general(qm, stb, (((1,), (1,)), ((), ())),
                                                   preferred_element_type=F32))
                    vh = vc[:, (2 * p + hh) * GLA_DV:(2 * p + hh + 1) * GLA_DV]
                    upd.append(jnp.dot(vh.T.astype(BF16), ke[:, sl], preferred_element_type=F32))
                st_scr[s, p] = dec[:, sl] * st + jnp.where(lane_sq < GLA_DK, upd[0], upd[1])
            o_ref[s, pl.ds(t0, GLA_SUB), :] += jnp.concatenate(o_parts, axis=-1)
        return carry

    lax.fori_loop(0, ts // GLA_SUB, step, 0)

    @pl.when(si == pl.num_programs(1) - 1)
    def _():
        stout_ref[...] = st_scr[...]


def _gla(gq, gk, gv, la, st0, p, ts=256):
    b, t, _ = gq.shape
    ts = min(ts, t)
    nb = 2 if b % 2 == 0 else 1
    i = jnp.arange(ts)
    tri = ((i[:, None] // GLA_SUB == i[None, :] // GLA_SUB) & (i[None, :] <= i[:, None])).astype(BF16)
    ones = _block_diag_rect(GLA_QW, GLA_DK, GLA_VW, GLA_DV)
    seq = lambda w: pl.BlockSpec((nb, ts, w), lambda bi, si: (bi, si, 0))
    stspec = pl.BlockSpec((nb, GLA_HEADS // 2, LANES, LANES), lambda bi, si: (bi, 0, 0, 0))
    return pl.pallas_call(
        functools.partial(_gla_kernel, ts=ts, nb=nb),
        grid=(b // nb, t // ts),
        in_specs=[seq(GLA_QW), seq(GLA_QW), seq(GLA_VW), seq(GLA_QW), stspec,
                  pl.BlockSpec((ts, ts), lambda bi, si: (0, 0)),
                  pl.BlockSpec((GLA_QW, GLA_VW), lambda bi, si: (0, 0))],
        out_specs=[seq(GLA_VW), stspec],
        out_shape=[jax.ShapeDtypeStruct((b, t, GLA_VW), F32),
                   jax.ShapeDtypeStruct((b, GLA_HEADS // 2, LANES, LANES), F32)],
        scratch_shapes=[pltpu.VMEM((nb, GLA_HEADS // 2, LANES, LANES), F32),
                        pltpu.VMEM((nb, ts, GLA_QW), F32)],
        compiler_params=_params(("parallel", "arbitrary"), VMEM_LIMIT),
        name="gla",
    )(gq, gk, gv, la, st0, tri, ones)


PAGES_PER_STEP = 8
NEW_T = 8


def _diff_sample_kernel(pt_ref, lam_ref, q_ref, *rest):
    del pt_ref
    np_ = PAGES_PER_STEP
    k_refs = rest[:np_]
    v_refs = rest[np_:2 * np_]
    kn_ref, vn_ref, on_ref, o_ref, m_scr, l_scr, acc_scr = rest[2 * np_:]
    i = pl.program_id(1)
    nt = (((1,), (1,)), ((), ()))

    @pl.when(i == 0)
    def _():
        m_scr[...] = jnp.full(m_scr.shape, -jnp.inf, F32)
        l_scr[...] = jnp.zeros(l_scr.shape, F32)
        acc_scr[...] = jnp.zeros(acc_scr.shape, F32)

    q = q_ref[0]

    def update(scores, value_products):
        m_old = m_scr[...]
        m_new = m_old
        for s in scores:
            m_new = jnp.maximum(m_new, jnp.max(s, axis=-1, keepdims=True))
        alpha = jnp.exp(m_old - m_new)
        l = alpha * l_scr[...]
        acc = alpha * acc_scr[...]
        for s, pv in zip(scores, value_products):
            e = jnp.exp(s - m_new)
            l = l + jnp.sum(e, axis=-1, keepdims=True)
            acc = acc + pv(e.astype(BF16))
        m_scr[...] = m_new
        l_scr[...] = l
        acc_scr[...] = acc

    def page_t(ref):
        return ref[...].reshape(DIFF_W, PAGE_SIZE).astype(BF16)

    scores = [jnp.dot(q, page_t(k_ref), preferred_element_type=F32) for k_ref in k_refs]
    update(scores, [lambda e, v_ref=v_ref: lax.dot_general(e, page_t(v_ref), nt,
                                                           preferred_element_type=F32)
                    for v_ref in v_refs])

    @pl.when(i == pl.num_programs(1) - 1)
    def _():
        s = lax.dot_general(q, kn_ref[0], nt, preferred_element_type=F32)
        t_q = lax.broadcasted_iota(I32, s.shape, 0) % NEW_T
        t_k = lax.broadcasted_iota(I32, s.shape, 1)
        update([jnp.where(t_k <= t_q, s, -jnp.inf)],
               [lambda e: jnp.dot(e, vn_ref[0], preferred_element_type=F32)])
        accn = acc_scr[...] / l_scr[...]
        rows = 2 * NEW_T
        lane = lax.broadcasted_iota(I32, (rows, DIFF_W), 1)
        o16 = jnp.zeros((rows, DIFF_W), F32)
        for h in range(DIFF_HEADS):
            o16 = jnp.where(lane // DIFF_V == h, accn[h * rows:(h + 1) * rows, :], o16)
        o = o16[:NEW_T] - lam_ref[0, 0] * o16[NEW_T:]
        lane = lax.broadcasted_iota(I32, (NEW_T, DIFF_W), 1)
        sq = o * o
        ms = jnp.zeros_like(o)
        for h in range(DIFF_HEADS):
            mine = lane // DIFF_V == h
            ms = jnp.where(mine, jnp.sum(jnp.where(mine, sq, 0.0), axis=-1, keepdims=True), ms)
        o_ref[0] = o * lax.rsqrt(ms * (1.0 / DIFF_V) + EPS) * on_ref[...]


def _diff_sample(lam, page_table, dq, dk16, dv16, cache_k, cache_v, p):
    db, t, _ = dq.shape
    n_pages = page_table.shape[1]
    np_ = PAGES_PER_STEP
    a = dq.reshape(db, t, 16, DIFF_QK).transpose(0, 2, 1, 3)
    q_bd = (a[:, :, :, None, :] * jnp.eye(16, dtype=BF16)[None, :, None, :, None]
            ).reshape(db, 16 * t, DIFF_W)
    pad = lambda x: jnp.pad(x, ((0, 0), (0, LANES - t), (0, 0)))
    ck = cache_k.transpose(0, 2, 3, 1)
    cv = cache_v.transpose(0, 2, 3, 1)

    def page_spec(n):
        return pl.BlockSpec((None, DIFF_HEADS, DIFF_V, PAGE_SIZE),
                            lambda b, i, pt: (pt[b, i * np_ + n], 0, 0, 0))

    per_b = lambda r: pl.BlockSpec((1, r, DIFF_W), lambda b, i, pt: (b, 0, 0))
    grid_spec = pltpu.PrefetchScalarGridSpec(
        num_scalar_prefetch=1,
        grid=(db, n_pages // np_),
        in_specs=[pl.BlockSpec(memory_space=pltpu.SMEM), per_b(16 * t)]
        + [page_spec(n) for n in range(np_)] + [page_spec(n) for n in range(np_)]
        + [per_b(LANES), per_b(LANES), pl.BlockSpec((1, DIFF_W), lambda b, i, pt: (0, 0))],
        out_specs=per_b(t),
        scratch_shapes=[pltpu.VMEM((16 * t, 1), F32), pltpu.VMEM((16 * t, 1), F32),
                        pltpu.VMEM((16 * t, DIFF_W), F32)],
    )
    return pl.pallas_call(
        _diff_sample_kernel,
        grid_spec=grid_spec,
        out_shape=jax.ShapeDtypeStruct((db, t, DIFF_W), F32),
        compiler_params=_params(("parallel", "arbitrary"), VMEM_LIMIT),
        name="diff_sample",
    )(page_table, lam, q_bd, *([ck] * np_), *([cv] * np_), pad(dk16), pad(dv16),
      p["diff_out_norm"])


def _top16_rows(s, row):
    vals, idxs = [], []
    sentinel = s.shape[0]
    for _ in range(PEER_TOPK):
        m = jnp.max(s, axis=0, keepdims=True)
        idx = jnp.min(jnp.where(s == m, row, sentinel), axis=0, keepdims=True)
        vals.append(m)
        idxs.append(idx)
        s = jnp.where(row == idx, -jnp.inf, s)
    return jnp.concatenate(vals, axis=0), jnp.concatenate(idxs, axis=0)


def _pick_row(table, row, idx):
    return jnp.max(jnp.where(row == idx, table, -1), axis=0, keepdims=True)


_CAND_WIDTH = [PEER_TOPK // (a + 1) for a in range(PEER_TOPK)]
_CAND_ROWS = -(-sum(_CAND_WIDTH) // 8) * 8


def _route_kernel(pq_ref, sk_ref, e_ref, g_ref, *, tb):
    nt = (((1,), (1,)), ((), ()))
    row_k = lax.broadcasted_iota(I32, (N_KEYS, tb), 0)
    row_c = lax.broadcasted_iota(I32, (_CAND_ROWS, tb), 0)
    n_pad = _CAND_ROWS - sum(_CAND_WIDTH)
    e_rows, g_rows = [], []
    for h in range(PEER_HEADS):
        tops = []
        for c in range(2):
            hc = 2 * h + c
            s = lax.dot_general(sk_ref[hc], pq_ref[:, hc * N_KEYS:(hc + 1) * N_KEYS], nt,
                                preferred_element_type=F32)
            tops.append(_top16_rows(s, row_k))
        (s1, i1), (s2, i2) = tops
        cand = jnp.concatenate(
            [jnp.broadcast_to(s1[a:a + 1], (nb, tb)) + s2[:nb] for a, nb in enumerate(_CAND_WIDTH)]
            + [jnp.full((n_pad, tb), -jnp.inf, F32)], axis=0)
        cand_e = jnp.concatenate(
            [jnp.broadcast_to(i1[a:a + 1], (nb, tb)) * N_KEYS + i2[:nb]
             for a, nb in enumerate(_CAND_WIDTH)] + [jnp.zeros((n_pad, tb), I32)], axis=0)
        top_s, j = _top16_rows(cand, row_c)
        e_rows.append(jnp.concatenate(
            [_pick_row(cand_e, row_c, j[r:r + 1]) for r in range(PEER_TOPK)], axis=0))
        ex = jnp.exp(top_s - top_s[0:1])
        g_rows.append(ex / jnp.sum(ex, axis=0, keepdims=True))
    e_ref[...] = (jnp.concatenate(e_rows, axis=0) * HALF_ROWS).T
    g_ref[...] = jnp.concatenate(g_rows, axis=0).T


def _route(pq, p, tb=128):
    n = pq.shape[0]
    tb = min(tb, n)
    return pl.pallas_call(
        functools.partial(_route_kernel, tb=tb),
        grid=(n // tb,),
        in_specs=[pl.BlockSpec((tb, 2048), lambda i: (i, 0)),
                  pl.BlockSpec((2 * PEER_HEADS, N_KEYS, N_KEYS), lambda i: (0, 0, 0))],
        out_specs=[pl.BlockSpec((tb, PEER_SEL), lambda i: (i, 0))] * 2,
        out_shape=[jax.ShapeDtypeStruct((n, PEER_SEL), I32), jax.ShapeDtypeStruct((n, PEER_SEL), F32)],
        compiler_params=_params(("parallel",), VMEM_LIMIT),
        name="peer_route",
    )(pq, p["peer_sub_keys"])


HALF_ROWS = 4


def _table_row(tab_ref, row_start):
    return tab_ref[pl.ds(pl.multiple_of(row_start, HALF_ROWS), HALF_ROWS), :]


def _unpack_pair(row):
    lo = lax.bitcast_convert_type(row << 16, F32)
    hi = lax.bitcast_convert_type(row & jnp.uint32(0xFFFF0000), F32)
    return lo, hi


def _gelu_tanh(x):
    return 0.5 * x * (1.0 + jnp.tanh(math.sqrt(2.0 / math.pi) * (x + 0.044715 * (x * x * x))))


def _gather_rows(e_ref, tab_ref, t, stage):
    for j in range(PEER_SEL):
        stage[j * HALF_ROWS:(j + 1) * HALF_ROWS, :] = _table_row(tab_ref, e_ref[t, j])


DOT_TOKENS = 4
SUM_TOKENS = 2


def _peer_dot_kernel(e_ref, xt_ref, g_ref, tab_ref, gsum_ref, w_ref, *scratch, tb):
    nt = DOT_TOKENS
    stages, parts = scratch[:nt], scratch[nt:]
    lane = lax.broadcasted_iota(I32, (PEER_SEL, tb), 1)
    gsum = gsum_ref[...]
    chunks = 2 * HALF_ROWS
    gsum16 = gsum_ref[:LANES // chunks, :LANES]
    pairs = PEER_SEL * chunks // 16

    def group_sums(t):
        prods = []
        for k, stage in enumerate(stages):
            x = xt_ref[jnp.maximum(t + k, 0)]
            rows = pltpu.bitcast(stage[...], BF16).reshape(pairs, 16, LANES)
            prods.append((rows * x[None]).reshape(PEER_SEL * chunks, LANES))
        per = LANES // chunks
        for c in range(PEER_SEL // per):
            for prod, part in zip(prods, parts):
                part[c * per:(c + 1) * per, :] = jnp.dot(
                    gsum16, prod[c * LANES:(c + 1) * LANES, :], preferred_element_type=F32)

    def lane_sums(t, ht):
        for k, part in enumerate(parts):
            ht = jnp.where(lane == t + k, jnp.sum(part[...], axis=1, keepdims=True), ht)
        return ht

    for ref in scratch:
        ref[...] = jnp.zeros(ref.shape, ref.dtype)

    def token_group(i, ht):
        t0 = nt * i
        ht = lane_sums(t0 - 2 * nt, ht)
        group_sums(t0 - nt)
        for k, stage in enumerate(stages):
            _gather_rows(e_ref, tab_ref, t0 + k, stage)
        return ht

    ht = lax.fori_loop(0, tb // nt, token_group, jnp.zeros((PEER_SEL, tb), F32))
    ht = lane_sums(tb - 2 * nt, ht)
    group_sums(tb - nt)
    ht = lane_sums(tb - nt, ht)
    w = g_ref[...] * _gelu_tanh(ht.T)
    w_ref[...] = jnp.dot(w.astype(BF16), gsum, preferred_element_type=F32)


def _peer_dot(e, xt3, g, tab, tb=128):
    n = e.shape[0]
    tb = min(tb, n)
    gsum = _block_diag_rect(PEER_SEL, 1, PEER_SEL * 2 * HALF_ROWS, 2 * HALF_ROWS)
    return pl.pallas_call(
        functools.partial(_peer_dot_kernel, tb=tb),
        grid=(n // tb,),
        in_specs=[pl.BlockSpec((tb, PEER_SEL), lambda i: (i, 0), memory_space=pltpu.SMEM),
                  pl.BlockSpec((tb, 16, LANES), lambda i: (i, 0, 0)),
                  pl.BlockSpec((tb, PEER_SEL), lambda i: (i, 0)),
                  pl.BlockSpec(memory_space=pltpu.VMEM),
                  pl.BlockSpec(gsum.shape, lambda i: (0, 0))],
        out_specs=pl.BlockSpec((tb, D_MODEL), lambda i: (i, 0)),
        out_shape=jax.ShapeDtypeStruct((n, D_MODEL), F32),
        scratch_shapes=[pltpu.VMEM((PEER_SEL * HALF_ROWS, LANES), jnp.uint32)] * DOT_TOKENS
        + [pltpu.VMEM((PEER_SEL, LANES), F32)] * DOT_TOKENS,
        compiler_params=_params(("arbitrary",), VMEM_LIMIT),
        name="peer_dot",
    )(e, xt3, g, tab, gsum)


def _peer_sum_kernel(e_ref, w_ref, h_ref, tab_ref, y_ref, *stages, tb):
    out_rows = 2 * HALF_ROWS
    q = lax.broadcasted_iota(I32, (2 * out_rows, D_MODEL), 0)
    lane = lax.broadcasted_iota(I32, (2 * out_rows, D_MODEL), 1)
    mine = (lane % out_rows == 2 * (q % HALF_ROWS) + q // HALF_ROWS) & (q < out_rows)

    def gather(t, stage):
        _gather_rows(e_ref, tab_ref, t, stage)

    def weighted_sum(t, stage):
        wrow = w_ref[pl.ds(t, 1), :]
        lhs = jnp.where(mine, wrow, 0.0).astype(BF16)
        rows = pltpu.bitcast(stage[...], BF16)
        out = jnp.dot(lhs, rows, preferred_element_type=F32)
        y_ref[t] = h_ref[t] + out[:out_rows]

    nt = len(stages)
    for stage in stages:
        stage[...] = jnp.zeros(stage.shape, stage.dtype)

    def token_group(i, carry):
        t0 = nt * i
        for k, stage in enumerate(stages):
            weighted_sum(jnp.maximum(t0 - nt + k, 0), stage)
        for k, stage in enumerate(stages):
            gather(t0 + k, stage)
        return carry

    lax.fori_loop(0, tb // nt, token_group, 0)
    for k, stage in enumerate(stages):
        weighted_sum(tb - nt + k, stage)


def _peer_sum(e, w, h3, tab, tb=128):
    n = e.shape[0]
    tb = min(tb, n)
    tok = lambda: pl.BlockSpec((tb, 8, LANES), lambda i: (i, 0, 0))
    stage = pltpu.VMEM((PEER_SEL * HALF_ROWS, LANES), jnp.uint32)
    return pl.pallas_call(
        functools.partial(_peer_sum_kernel, tb=tb),
        grid=(n // tb,),
        in_specs=[pl.BlockSpec((tb, PEER_SEL), lambda i: (i, 0), memory_space=pltpu.SMEM),
                  pl.BlockSpec((tb, D_MODEL), lambda i: (i, 0)), tok(),
                  pl.BlockSpec(memory_space=pltpu.VMEM)],
        out_specs=tok(),
        out_shape=jax.ShapeDtypeStruct((n, 8, LANES), F32),
        scratch_shapes=[stage] * SUM_TOKENS,
        compiler_params=_params(("arbitrary",), VMEM_LIMIT),
        name="peer_sum",
    )(e, w, h3, tab)


def _pack_table(w):
    bits = lax.bitcast_convert_type(w.astype(BF16), jnp.uint16).astype(jnp.uint32)
    half = D_MODEL // 2
    return (bits[:, :half] | (bits[:, half:] << 16)).reshape(w.shape[0] * HALF_ROWS, LANES)


def _block_diag_rect(rows, rgroup, cols, cgroup):
    r = jnp.arange(rows) // rgroup
    c = jnp.arange(cols) // cgroup
    return (r[:, None] == c[None, :]).astype(BF16)


def _state_to_pairs(s):
    b = s.shape[0]
    return s.reshape(b, 2, 2, GLA_DK, GLA_DV).transpose(0, 1, 4, 2, 3).reshape(b, 2, LANES, LANES)


def _pairs_to_state(st):
    b = st.shape[0]
    return st.reshape(b, 2, GLA_DV, 2, GLA_DK).transpose(0, 1, 3, 4, 2).reshape(
        b, GLA_HEADS, GLA_DK, GLA_DV)


def _lambda_init(layer):
    return 0.8 - 0.6 * math.exp(-0.3 * layer)


def _prep_layer(l, w):
    row = lambda a: a.reshape(1, -1).astype(F32)
    w_in = w["w_in"][l]
    lam_init = _lambda_init(l)
    return {
        "attn_norm": row(w["attn_norm"][l]),
        "w_a": w_in[:, :_OFF_GA].astype(BF16),
        "w_ga": jnp.pad(w_in[:, _OFF_GA:_OFF_GR], ((0, 0), (0, LANES - GLA_GATE_RANK))).astype(BF16),
        "w_gr": w_in[:, _OFF_GR:].astype(BF16),
        "dq_norm": row(jnp.tile(w["diff_q_norm"][l], 2 * DIFF_HEADS)) * (DIFF_QK ** -0.5),
        "dk_norm": row(jnp.tile(w["diff_k_norm"][l], 2 * DIFF_HEADS)),
        "bd32": _block_diag(DIFF_W, DIFF_QK, 1.0 / DIFF_QK),
        "w_gate": jnp.pad(w["gla_w_gate"][l], ((0, LANES - GLA_GATE_RANK), (0, 0))).astype(BF16),
        "b_gate": row(w["gla_b_gate"][l]),
        "diff_out_norm": row(jnp.tile(w["diff_out_norm"][l], DIFF_HEADS)) * (1.0 - lam_init),
        "gla_out_norm": row(jnp.tile(w["gla_out_norm"][l], GLA_HEADS)),
        "w_o_d": w["w_o"][l][:DIFF_W].astype(BF16),
        "w_o_g": w["w_o"][l][DIFF_W:].astype(BF16),
        "cross_norm": row(w["cross_norm"][l]),
        "cross_w_q": w["cross_w_q"][l].astype(BF16),
        "cross_q_norm": row(jnp.tile(w["cross_q_norm"][l], CROSS_HEADS)) * (CROSS_HD ** -0.5),
        "mem_norm": row(w["mem_norm"][l]),
        "cross_w_kv": w["cross_w_kv"][l].astype(BF16),
        "cross_k_norm": row(jnp.tile(w["cross_k_norm"][l], CROSS_HEADS)),
        "cross_w_o": w["cross_w_o"][l].astype(BF16),
        "ffn_norm": row(w["ffn_norm"][l]),
        "peer_w_q": w["peer_w_q"][l].astype(BF16),
        "peer_q_norm": row(jnp.tile(w["peer_q_norm"][l], PEER_HEADS)),
        "peer_sub_keys": w["peer_sub_keys"][l].reshape(2 * PEER_HEADS, N_KEYS, N_KEYS).astype(BF16),
        "peer_u": _pack_table(w["peer_u"][l]),
        "peer_v": _pack_table(w["peer_v"][l]),
    }


def _diff_lambda(w, l):
    f = lambda a: a[l].astype(F32)
    lam = (jnp.exp(jnp.sum(f(w["lambda_q1"]) * f(w["lambda_k1"])))
           - jnp.exp(jnp.sum(f(w["lambda_q2"]) * f(w["lambda_k2"]))) + _lambda_init(l))
    return lam.reshape(1, 1)


def _peer(h2, xt, pq, p):
    n = h2.shape[0]
    e, g = _route(pq, p)
    w = _peer_dot(e, xt.reshape(n, 16, LANES), g, p["peer_u"])
    return _peer_sum(e, w, h2.reshape(n, 8, LANES), p["peer_v"]).reshape(n, D_MODEL)


def _tail(od, og, gr, x, mk, mv, p):
    b, t, _ = x.shape
    n = b * t
    flat = lambda a: a.reshape(n, a.shape[-1])
    h1, qc = _mix_out(flat(od), flat(og), flat(gr), flat(x), p)
    h2, xt, pq = _cross(qc.reshape(b, t, D_MODEL), mk, mv, h1.reshape(b, t, D_MODEL), p)
    return _peer(flat(h2), flat(xt), flat(pq), p).reshape(b, t, D_MODEL)


def kernel(x_prompt, x_sample, mem_prompt, cache_diff_k, cache_diff_v, page_table, state_gla,
           cache_mem_k, cache_mem_v, attn_norm, w_in, diff_q_norm, diff_k_norm, lambda_q1,
           lambda_k1, lambda_q2, lambda_k2, diff_out_norm, gla_w_gate, gla_b_gate, gla_out_norm,
           w_o, cross_norm, mem_norm, cross_w_q, cross_w_kv, cross_q_norm, cross_k_norm,
           cross_w_o, ffn_norm, peer_w_q, peer_q_norm, peer_sub_keys, peer_u, peer_v):
    w = dict(attn_norm=attn_norm, w_in=w_in, diff_q_norm=diff_q_norm, diff_k_norm=diff_k_norm,
             lambda_q1=lambda_q1, lambda_k1=lambda_k1, lambda_q2=lambda_q2, lambda_k2=lambda_k2,
             diff_out_norm=diff_out_norm, gla_w_gate=gla_w_gate, gla_b_gate=gla_b_gate,
             gla_out_norm=gla_out_norm, w_o=w_o, cross_norm=cross_norm, mem_norm=mem_norm,
             cross_w_q=cross_w_q, cross_w_kv=cross_w_kv, cross_q_norm=cross_q_norm,
             cross_k_norm=cross_k_norm, cross_w_o=cross_w_o, ffn_norm=ffn_norm,
             peer_w_q=peer_w_q, peer_q_norm=peer_q_norm, peer_sub_keys=peer_sub_keys,
             peer_u=peer_u, peer_v=peer_v)
    b, s, _ = x_prompt.shape
    db, t, _ = x_sample.shape
    depth = w_in.shape[0]
    hp, hs = x_prompt, x_sample
    outs = [[] for _ in range(8)]
    for l in range(depth):
        p = _prep_layer(l, w)
        lam = _diff_lambda(w, l)

        dq, dk, dv, dk16, dv16, gq, gk, gv, la, gr = _proj_in(hp.reshape(b * s, D_MODEL), p)
        seq = lambda a: a.reshape(b, s, a.shape[-1])
        od = _diff_prompt(lam, seq(dq), seq(dk16), seq(dv16), p)
        st0 = jnp.zeros((b, GLA_HEADS // 2, LANES, LANES), F32)
        og, st = _gla(seq(gq), seq(gk), seq(gv), seq(la), st0, p)
        mk, mv = _mem_kv(mem_prompt.reshape(b * N_MEM, D_MODEL), p)
        mk = mk.reshape(b, N_MEM, D_MODEL)
        mv = mv.reshape(b, N_MEM, D_MODEL)
        hp = _tail(od, og, seq(gr), hp, mk, mv, p)
        outs[0].append(dk.reshape(b, s, DIFF_HEADS, DIFF_V))
        outs[1].append(dv.reshape(b, s, DIFF_HEADS, DIFF_V))
        outs[2].append(_pairs_to_state(st))
        outs[3].append(mk.reshape(b, N_MEM, CROSS_HEADS, CROSS_HD))
        outs[4].append(mv.reshape(b, N_MEM, CROSS_HEADS, CROSS_HD))

        dq, dk, dv, dk16, dv16, gq, gk, gv, la, gr = _proj_in(hs.reshape(db * t, D_MODEL), p)
        seq = lambda a: a.reshape(db, t, a.shape[-1])
        od = _diff_sample(lam, page_table, seq(dq), seq(dk16), seq(dv16), cache_diff_k[l],
                          cache_diff_v[l], p)
        tp = -(-t // GLA_SUB) * GLA_SUB
        padt = lambda a: jnp.pad(seq(a), ((0, 0), (0, tp - t), (0, 0)))
        og, st = _gla(padt(gq), padt(gk), padt(gv), padt(la), _state_to_pairs(state_gla[l]), p)
        hs_pad = jnp.pad(hs, ((0, 0), (0, tp - t), (0, 0)))
        y = _tail(jnp.pad(od, ((0, 0), (0, tp - t), (0, 0))).astype(BF16), og, padt(gr), hs_pad,
                  cache_mem_k[l].reshape(db, N_MEM, D_MODEL),
                  cache_mem_v[l].reshape(db, N_MEM, D_MODEL), p)
        hs = y[:, :t]
        outs[5].append(dk.reshape(db, t, DIFF_HEADS, DIFF_V))
        outs[6].append(dv.reshape(db, t, DIFF_HEADS, DIFF_V))
        outs[7].append(_pairs_to_state(st))

    stack = lambda xs: jnp.stack(xs)
    return (hp, hs, stack(outs[0]), stack(outs[1]), stack(outs[2]), stack(outs[3]), stack(outs[4]),
            stack(outs[5]), stack(outs[6]), stack(outs[7]))
```
